```python
import jax, jax.numpy as jnp
from jax import lax
import numpy as np

D_MODEL = 4096
BATCH = 1
SEQ = 8192
DEPTH = 1

MEM_LEN = 256

POOL_WINDOWS = (2, 4, 8, 16)
POOL_GROUPS = 4
POOL_WIDTH = D_MODEL
POOL_GROUP_DIM = POOL_WIDTH // POOL_GROUPS

RET_QK_DIM = 256
RET_HEADS = D_MODEL // RET_QK_DIM
RET_V_DIM = 2 * RET_QK_DIM
RET_QK_WIDTH = RET_HEADS * RET_QK_DIM
RET_V_WIDTH = RET_HEADS * RET_V_DIM
RET_CHUNK = 128

MEM_HEADS = 4
MEM_WIDTH = D_MODEL
MEM_HEAD_DIM = MEM_WIDTH // MEM_HEADS

ROPE_BASE = 10000.0
NORM_EPS = 1e-6

IN_SPLITS = (POOL_WIDTH, POOL_WIDTH,
             RET_QK_WIDTH, RET_QK_WIDTH, RET_V_WIDTH, RET_V_WIDTH,
             MEM_WIDTH, MEM_WIDTH,
             D_MODEL, D_MODEL, D_MODEL)
IN_WIDTH = sum(IN_SPLITS)
IN_OFFSETS = tuple(int(o) for o in np.cumsum(IN_SPLITS)[:-1])

kernel_name = "hybrid_pool_retention_memory_gated_block"


def rmsnorm(x, g):
    xf = x.astype(jnp.float32)
    y = xf * lax.rsqrt(jnp.mean(xf * xf, axis=-1, keepdims=True) + NORM_EPS)
    return (y * g.astype(jnp.float32)).astype(x.dtype)


def causal_multiscale_pool(u, w_group, scale):
    B, S, W = u.shape
    uf = u.astype(jnp.float32)
    c = jnp.concatenate([jnp.zeros((B, 1, W), jnp.float32), jnp.cumsum(uf, axis=1)], axis=1)
    c = c.reshape(B, S + 1, POOL_GROUPS, POOL_GROUP_DIM)
    t = jnp.arange(S)
    pooled = []
    for gi, w in enumerate(POOL_WINDOWS):
        cg = c[:, :, gi]
        lo = cg[:, jnp.maximum(t + 1 - w, 0)]
        cnt = jnp.minimum(t + 1, w).astype(jnp.float32)
        pooled.append((cg[:, 1:] - lo) / cnt[None, :, None])
    pooled = jnp.stack(pooled, axis=2)
    mixed = pooled - uf.reshape(B, S, POOL_GROUPS, POOL_GROUP_DIM)
    y = jnp.einsum('bsgd,gde->bsge', mixed, w_group.astype(jnp.float32))
    y = y.reshape(B, S, W) * scale.astype(jnp.float32)
    return y.astype(u.dtype)


def rope(x, pos):
    half = x.shape[-1] // 2
    inv = ROPE_BASE ** (-jnp.arange(half, dtype=jnp.float32) / half)
    ang = pos[:, None] * inv[None, :]
    cos = jnp.cos(ang)[None, :, None, :]
    sin = jnp.sin(ang)[None, :, None, :]
    xf = x.astype(jnp.float32)
    x1, x2 = xf[..., :half], xf[..., half:]
    return jnp.concatenate([x1 * cos - x2 * sin, x2 * cos + x1 * sin], axis=-1)


def chunkwise_retention(q, k, v):
    B, S, H, dk = q.shape
    dv = v.shape[-1]
    C = RET_CHUNK
    N = S // C
    lg = jnp.log1p(-(2.0 ** (-5.0 - jnp.arange(H, dtype=jnp.float32))))
    idx = jnp.arange(C, dtype=jnp.float32)
    diff = idx[:, None] - idx[None, :]
    intra = jnp.where(diff >= 0, jnp.exp(jnp.maximum(diff, 0.0)[None] * lg[:, None, None]), 0.0)
    q_decay = jnp.exp((idx + 1.0)[None, :] * lg[:, None])
    k_decay = jnp.exp((C - 1.0 - idx)[None, :] * lg[:, None])
    chunk_decay = jnp.exp(C * lg)

    def to_chunks(a):
        d = a.shape[-1]
        return a.astype(jnp.float32).reshape(B, N, C, H, d).transpose(1, 0, 3, 2, 4)

    def step(state, xs):
        qc, kc, vc = xs
        scores = jnp.einsum('bhid,bhjd->bhij', qc, kc) * intra[None]
        inner = jnp.einsum('bhij,bhje->bhie', scores, vc)
        cross = jnp.einsum('bhid,bhde->bhie', qc * q_decay[None, :, :, None], state)
        new_state = state * chunk_decay[None, :, None, None] + jnp.einsum(
            'bhjd,bhje->bhde', kc * k_decay[None, :, :, None], vc)
        return new_state, inner + cross

    s0 = jnp.zeros((B, H, dk, dv), jnp.float32)
    _, o = lax.scan(step, s0, (to_chunks(q), to_chunks(k), to_chunks(v)))
    return o.transpose(1, 0, 3, 2, 4).reshape(B, S, H, dv)


def memory_cross_attention(qm, km, vm):
    scores = jnp.einsum('bshd,bmhd->bhsm', qm.astype(jnp.float32), km.astype(jnp.float32))
    p = jax.nn.softmax(scores * (MEM_HEAD_DIM ** -0.5), axis=-1)
    return jnp.einsum('bhsm,bmhd->bshd', p.astype(vm.dtype), vm)


def setup_inputs(seed: int = 0) -> dict:
    key = jax.random.key(seed)
    ks = jax.random.split(key, 16)
    f32 = jnp.float32

    def nrm(k, shape, fan_in):
        return jax.random.normal(k, shape, f32) * (fan_in ** -0.5)

    return {
        "x": jax.random.normal(ks[0], (BATCH, SEQ, D_MODEL), f32),
        "mem": jax.random.normal(ks[1], (BATCH, MEM_LEN, D_MODEL), f32),
        "norm_in": 1.0 + 0.02 * jax.random.normal(ks[2], (DEPTH, D_MODEL), f32),
        "norm_mem": 1.0 + 0.02 * jax.random.normal(ks[3], (DEPTH, D_MODEL), f32),
        "w_in": nrm(ks[4], (DEPTH, D_MODEL, IN_WIDTH), D_MODEL),
        "w_pool_group": nrm(ks[5], (DEPTH, POOL_GROUPS, POOL_GROUP_DIM, POOL_GROUP_DIM), POOL_GROUP_DIM),
        "pool_scale": 1.0 + 0.02 * jax.random.normal(ks[6], (DEPTH, POOL_WIDTH), f32),
        "w_mem_k": nrm(ks[7], (DEPTH, D_MODEL, MEM_WIDTH), D_MODEL),
        "w_mem_v": nrm(ks[8], (DEPTH, D_MODEL, MEM_WIDTH), D_MODEL),
        "w_proj_pool": nrm(ks[9], (DEPTH, POOL_WIDTH, D_MODEL), POOL_WIDTH),
        "w_proj_ret": nrm(ks[10], (DEPTH, RET_V_WIDTH, D_MODEL), RET_V_WIDTH),
        "w_proj_mem": nrm(ks[11], (DEPTH, MEM_WIDTH, D_MODEL), MEM_WIDTH),
        "w_out": nrm(ks[12], (DEPTH, D_MODEL, D_MODEL), D_MODEL),
        "norm_f": 1.0 + 0.02 * jax.random.normal(ks[13], (D_MODEL,), f32),
    }


def reference(x, mem, norm_in, norm_mem, w_in, w_pool_group, pool_scale, w_mem_k, w_mem_v,
              w_proj_pool, w_proj_ret, w_proj_mem, w_out, norm_f):
    B, S, _ = x.shape
    M = mem.shape[1]
    pos = jnp.arange(S, dtype=jnp.float32)
    for l in range(DEPTH):
        h = rmsnorm(x, norm_in[l])
        z = h @ w_in[l]
        (u_pool, g_pool, q, k, v, g_ret, q_mem, g_mem,
         a_pool, a_ret, a_mem) = jnp.split(z, IN_OFFSETS, axis=-1)

        pool_out = causal_multiscale_pool(u_pool, w_pool_group[l], pool_scale[l]) * jax.nn.silu(g_pool)
        branch_pool = pool_out @ w_proj_pool[l]

        qh = rope(q.reshape(B, S, RET_HEADS, RET_QK_DIM), pos)
        kh = rope(k.reshape(B, S, RET_HEADS, RET_QK_DIM), pos) * (RET_QK_DIM ** -0.5)
        vh = v.reshape(B, S, RET_HEADS, RET_V_DIM)
        o = chunkwise_retention(qh, kh, vh)
        o = o * lax.rsqrt(jnp.mean(o * o, axis=-1, keepdims=True) + NORM_EPS)
        ret_out = o.reshape(B, S, RET_V_WIDTH).astype(x.dtype) * jax.nn.silu(g_ret)
        branch_ret = ret_out @ w_proj_ret[l]

        memn = rmsnorm(mem, norm_mem[l])
        km = (memn @ w_mem_k[l]).reshape(B, M, MEM_HEADS, MEM_HEAD_DIM)
        vm = (memn @ w_mem_v[l]).reshape(B, M, MEM_HEADS, MEM_HEAD_DIM)
        mo = memory_cross_attention(q_mem.reshape(B, S, MEM_HEADS, MEM_HEAD_DIM), km, vm)
        mem_out = mo.reshape(B, S, MEM_WIDTH) * jax.nn.silu(g_mem)
        branch_mem = mem_out @ w_proj_mem[l]

        merged = (jax.nn.sigmoid(a_pool) * branch_pool
                  + jax.nn.sigmoid(a_ret) * branch_ret
                  + jax.nn.sigmoid(a_mem) * branch_mem)
        x = x + merged @ w_out[l]
    return rmsnorm(x, norm_f)
```

```python
import functools

import jax
import jax.numpy as jnp
from jax import lax
from jax.experimental import pallas as pl
from jax.experimental.pallas import tpu as pltpu

D_MODEL = 4096
POOL_WINDOWS = (2, 4, 8, 16)
POOL_GROUPS = 4
POOL_GROUP_DIM = D_MODEL // POOL_GROUPS
POOL_HALO = 16
RET_QK_DIM = 256
RET_HEADS = D_MODEL // RET_QK_DIM
RET_V_DIM = 2 * RET_QK_DIM
RET_V_WIDTH = RET_HEADS * RET_V_DIM
MEM_HEADS = 4
MEM_HEAD_DIM = D_MODEL // MEM_HEADS
ROPE_BASE = 10000.0
ROPE_HALF = RET_QK_DIM // 2
NORM_EPS = 1e-6

SEG_U_POOL, SEG_G_POOL, SEG_Q, SEG_K, SEG_V, SEG_G_RET, SEG_Q_MEM, SEG_G_MEM, SEG_A_POOL, SEG_A_RET, SEG_A_MEM = (
    0, 1, 2, 3, 4, 6, 8, 9, 10, 11, 12)
IN_WIDTH = 13 * D_MODEL

F32 = jnp.float32
BF16 = jnp.bfloat16
VMEM_LIMIT = 56 * 1024 * 1024


def _params(*sem):
    return pltpu.CompilerParams(dimension_semantics=sem, vmem_limit_bytes=VMEM_LIMIT)


def _sigmoid(x):
    return 1.0 / (1.0 + jnp.exp(-x))


def _rmsnorm_kernel(x_ref, g_ref, o_ref):
    x = x_ref[...]
    y = x * lax.rsqrt(jnp.mean(x * x, axis=-1, keepdims=True) + NORM_EPS)
    o_ref[...] = (y * g_ref[...]).astype(o_ref.dtype)


def _rmsnorm(x, g, tm):
    rows, d = x.shape
    return pl.pallas_call(
        _rmsnorm_kernel,
        grid=(rows // tm,),
        in_specs=[pl.BlockSpec((tm, d), lambda i: (i, 0)),
                  pl.BlockSpec((1, d), lambda i: (0, 0))],
        out_specs=pl.BlockSpec((tm, d), lambda i: (i, 0)),
        out_shape=jax.ShapeDtypeStruct((rows, d), BF16),
        compiler_params=_params("parallel"),
        name="rmsnorm",
    )(x, g.reshape(1, d))


def _in_proj_kernel(h_ref, w_ref, cos_ref, sin_ref, o_ref, *, tn):
    seg = pl.program_id(1) // (D_MODEL // tn)
    acc = jnp.dot(h_ref[...], w_ref[...], preferred_element_type=F32)

    def is_in(*segs):
        pred = seg == segs[0]
        for s in segs[1:]:
            pred = pred | (seg == s)
        return pred

    @pl.when(is_in(SEG_U_POOL, SEG_V, SEG_V + 1, SEG_Q_MEM))
    def _():
        o_ref[...] = acc.astype(o_ref.dtype)

    @pl.when(is_in(SEG_G_POOL, SEG_G_RET, SEG_G_RET + 1, SEG_G_MEM))
    def _():
        o_ref[...] = (acc * _sigmoid(acc)).astype(o_ref.dtype)

    @pl.when(seg >= SEG_A_POOL)
    def _():
        o_ref[...] = _sigmoid(acc).astype(o_ref.dtype)

    def rope(scale):
        cos = cos_ref[...]
        sin = sin_ref[...]
        for hh in range(tn // RET_QK_DIM):
            lo = hh * RET_QK_DIM
            x1 = acc[:, lo:lo + ROPE_HALF]
            x2 = acc[:, lo + ROPE_HALF:lo + RET_QK_DIM]
            o_ref[:, lo:lo + ROPE_HALF] = ((x1 * cos - x2 * sin) * scale).astype(o_ref.dtype)
            o_ref[:, lo + ROPE_HALF:lo + RET_QK_DIM] = ((x2 * cos + x1 * sin) * scale).astype(o_ref.dtype)

    @pl.when(seg == SEG_Q)
    def _():
        rope(1.0)

    @pl.when(seg == SEG_K)
    def _():
        rope(RET_QK_DIM ** -0.5)


def _in_proj(h, w, cos, sin, tm, tn):
    s, d = h.shape
    n = w.shape[1]
    return pl.pallas_call(
        functools.partial(_in_proj_kernel, tn=tn),
        grid=(s // tm, n // tn),
        in_specs=[pl.BlockSpec((tm, d), lambda i, j: (i, 0)),
                  pl.BlockSpec((d, tn), lambda i, j: (0, j)),
                  pl.BlockSpec((tm, ROPE_HALF), lambda i, j: (i, 0)),
                  pl.BlockSpec((tm, ROPE_HALF), lambda i, j: (i, 0))],
        out_specs=pl.BlockSpec((tm, tn), lambda i, j: (i, j)),
        out_shape=jax.ShapeDtypeStruct((s, n), BF16),
        compiler_params=_params("parallel", "arbitrary"),
        name="in_proj",
    )(h, w, cos, sin)


def _matmul_kernel(a_ref, b_ref, o_ref):
    o_ref[...] = jnp.dot(a_ref[...], b_ref[...], preferred_element_type=F32).astype(o_ref.dtype)


def _matmul(a, b, tm, tn):
    m, k = a.shape
    n = b.shape[1]
    return pl.pallas_call(
        _matmul_kernel,
        grid=(m // tm, n // tn),
        in_specs=[pl.BlockSpec((tm, k), lambda i, j: (i, 0)),
                  pl.BlockSpec((k, tn), lambda i, j: (0, j))],
        out_specs=pl.BlockSpec((tm, tn), lambda i, j: (i, j)),
        out_shape=jax.ShapeDtypeStruct((m, n), BF16),
        compiler_params=_params("parallel", "arbitrary"),
        name="mem_kv_proj",
    )(a, b)


def _pool_kernel(u_ref, halo_ref, gate_ref, w_ref, scale_ref, o_ref, buf_ref, *, tm):
    g = pl.program_id(0)
    i = pl.program_id(1)
    u = u_ref[...].astype(F32)
    halo = halo_ref[...].astype(F32)
    buf_ref[0:POOL_HALO, :] = jnp.where(i == 0, 0.0, halo)
    buf_ref[POOL_HALO:POOL_HALO + tm, :] = u
    t = i * tm + lax.broadcasted_iota(jnp.int32, (tm, 1), 0)

    for gi, win in enumerate(POOL_WINDOWS):
        @pl.when(g == gi)
        def _(win=win):
            wsum = u
            for back in range(1, win):
                wsum = wsum + buf_ref[POOL_HALO - back:POOL_HALO - back + tm, :]
            cnt = jnp.minimum(t + 1, win).astype(F32)
            mixed = wsum / cnt - u
            y = jnp.dot(mixed.astype(BF16), w_ref[0], preferred_element_type=F32)
            y = y * scale_ref[...] * gate_ref[...].astype(F32)
            o_ref[...] = y.astype(o_ref.dtype)


def _pool_branch(z, w_group, scale, tm):
    s = z.shape[0]
    gd = POOL_GROUP_DIM
    halo_blocks_per_tile = tm // POOL_HALO
    return pl.pallas_call(
        functools.partial(_pool_kernel, tm=tm),
        grid=(POOL_GROUPS, s // tm),
        in_specs=[
            pl.BlockSpec((tm, gd), lambda g, i: (i, SEG_U_POOL * POOL_GROUPS + g)),
            pl.BlockSpec((POOL_HALO, gd),
                         lambda g, i: (jnp.maximum(i * halo_blocks_per_tile - 1, 0), SEG_U_POOL * POOL_GROUPS + g)),
            pl.BlockSpec((tm, gd), lambda g, i: (i, SEG_G_POOL * POOL_GROUPS + g)),
            pl.BlockSpec((1, gd, gd), lambda g, i: (g, 0, 0)),
            pl.BlockSpec((1, gd), lambda g, i: (0, g)),
        ],
        out_specs=pl.BlockSpec((tm, gd), lambda g, i: (i, g)),
        out_shape=jax.ShapeDtypeStruct((s, D_MODEL), BF16),
        scratch_shapes=[pltpu.VMEM((tm + POOL_HALO, gd), F32)],
        compiler_params=_params("parallel", "arbitrary"),
        name="pool_branch",
    )(z, z, z, w_group, scale.reshape(1, D_MODEL))


def _retention_kernel(lg_ref, q_ref, k_ref, v_ref, gate_ref, o_ref, state_ref, *, chunk):
    n = pl.program_id(1)

    @pl.when(n == 0)
    def _():
        state_ref[...] = jnp.zeros_like(state_ref)

    lg = lg_ref[0][0:1, 0:1]
    row = lax.broadcasted_iota(jnp.int32, (chunk, 1), 0).astype(F32)
    col = lax.broadcasted_iota(jnp.int32, (1, chunk), 1).astype(F32)
    diff = row - col
    intra = jnp.where(diff >= 0, jnp.exp(jnp.maximum(diff, 0.0) * lg), 0.0)
    q_decay = jnp.exp((row + 1.0) * lg)
    k_decay = jnp.exp((chunk - 1.0 - row) * lg)
    chunk_decay = jnp.exp(chunk * lg)

    q = q_ref[...]
    k = k_ref[...]
    v = v_ref[...]
    state = state_ref[...]
    scores = lax.dot_general(q, k, (((1,), (1,)), ((), ())), preferred_element_type=F32) * intra
    inner = jnp.dot(scores.astype(BF16), v, preferred_element_type=F32)
    cross = jnp.dot(q, state.astype(BF16), preferred_element_type=F32) * q_decay
    kd = (k.astype(F32) * k_decay).astype(BF16)
    state_ref[...] = state * chunk_decay + lax.dot_general(
        kd, v, (((0,), (0,)), ((), ())), preferred_element_type=F32)

    o = inner + cross
    o = o * lax.rsqrt(jnp.mean(o * o, axis=-1, keepdims=True) + NORM_EPS)
    o_ref[...] = (o * gate_ref[...].astype(F32)).astype(o_ref.dtype)


def _retention_branch(z, lg_tab, chunk):
    s = z.shape[0]
    q_blk = SEG_Q * D_MODEL // RET_QK_DIM
    k_blk = SEG_K * D_MODEL // RET_QK_DIM
    v_blk = SEG_V * D_MODEL // RET_V_DIM
    g_blk = SEG_G_RET * D_MODEL // RET_V_DIM
    return pl.pallas_call(
        functools.partial(_retention_kernel, chunk=chunk),
        grid=(RET_HEADS, s // chunk),
        in_specs=[
            pl.BlockSpec((1, 8, 128), lambda h, n: (h, 0, 0)),
            pl.BlockSpec((chunk, RET_QK_DIM), lambda h, n: (n, q_blk + h)),
            pl.BlockSpec((chunk, RET_QK_DIM), lambda h, n: (n, k_blk + h)),
            pl.BlockSpec((chunk, RET_V_DIM), lambda h, n: (n, v_blk + h)),
            pl.BlockSpec((chunk, RET_V_DIM), lambda h, n: (n, g_blk + h)),
        ],
        out_specs=pl.BlockSpec((chunk, RET_V_DIM), lambda h, n: (n, h)),
        out_shape=jax.ShapeDtypeStruct((s, RET_V_WIDTH), BF16),
        scratch_shapes=[pltpu.VMEM((RET_QK_DIM, RET_V_DIM), F32)],
        compiler_params=_params("parallel", "arbitrary"),
        name="retention_branch",
    )(lg_tab, z, z, z, z)


def _mem_attn_kernel(q_ref, gate_ref, k_ref, v_ref, o_ref):
    scores = lax.dot_general(q_ref[...], k_ref[...], (((1,), (1,)), ((), ())),
                             preferred_element_type=F32) * (MEM_HEAD_DIM ** -0.5)
    e = jnp.exp(scores - jnp.max(scores, axis=-1, keepdims=True))
    p = e * (1.0 / jnp.sum(e, axis=-1, keepdims=True))
    mo = jnp.dot(p.astype(BF16), v_ref[...], preferred_element_type=F32)
    o_ref[...] = (mo * gate_ref[...].astype(F32)).astype(o_ref.dtype)


def _mem_branch(z, km, vm, tm):
    s = z.shape[0]
    m = km.shape[0]
    hd = MEM_HEAD_DIM
    return pl.pallas_call(
        _mem_attn_kernel,
        grid=(MEM_HEADS, s // tm),
        in_specs=[
            pl.BlockSpec((tm, hd), lambda h, i: (i, SEG_Q_MEM * MEM_HEADS + h)),
            pl.BlockSpec((tm, hd), lambda h, i: (i, SEG_G_MEM * MEM_HEADS + h)),
            pl.BlockSpec((m, hd), lambda h, i: (0, h)),
            pl.BlockSpec((m, hd), lambda h, i: (0, h)),
        ],
        out_specs=pl.BlockSpec((tm, hd), lambda h, i: (i, h)),
        out_shape=jax.ShapeDtypeStruct((s, D_MODEL), BF16),
        compiler_params=_params("parallel", "arbitrary"),
        name="mem_branch",
    )(z, z, km, vm)


def _gated_proj_kernel(a_ref, w_ref, gate_ref, *rest, has_prev):
    if has_prev:
        prev_ref, o_ref = rest
    else:
        (o_ref,) = rest
    y = jnp.dot(a_ref[...], w_ref[...], preferred_element_type=F32) * gate_ref[...].astype(F32)
    if has_prev:
        y = y + prev_ref[...].astype(F32)
    o_ref[...] = y.astype(o_ref.dtype)


def _gated_proj(a, w, z, gate_seg, prev, tm, tn, out_dtype):
    s, k = a.shape
    n = w.shape[1]
    gate_blk = gate_seg * D_MODEL // tn
    in_specs = [pl.BlockSpec((tm, k), lambda i, j: (i, 0)),
                pl.BlockSpec((k, tn), lambda i, j: (0, j)),
                pl.BlockSpec((tm, tn), lambda i, j: (i, gate_blk + j))]
    args = [a, w, z]
    if prev is not None:
        in_specs.append(pl.BlockSpec((tm, tn), lambda i, j: (i, j)))
        args.append(prev)
    return pl.pallas_call(
        functools.partial(_gated_proj_kernel, has_prev=prev is not None),
        grid=(s // tm, n // tn),
        in_specs=in_specs,
        out_specs=pl.BlockSpec((tm, tn), lambda i, j: (i, j)),
        out_shape=jax.ShapeDtypeStruct((s, n), out_dtype),
        compiler_params=_params("parallel", "arbitrary"),
        name="gated_proj",
    )(*args)


def _out_proj_kernel(m_ref, w_ref, x_ref, g_ref, o_ref, *, tn):
    j = pl.program_id(1)
    col = pl.multiple_of(j * tn, tn)
    o_ref[:, pl.ds(col, tn)] = x_ref[...] + jnp.dot(m_ref[...], w_ref[...], preferred_element_type=F32)

    @pl.when(j == pl.num_programs(1) - 1)
    def _():
        y = o_ref[...]
        y = y * lax.rsqrt(jnp.mean(y * y, axis=-1, keepdims=True) + NORM_EPS)
        o_ref[...] = y * g_ref[...]


def _out_proj(merged, w, x, g, tm, tn):
    s, d = x.shape
    return pl.pallas_call(
        functools.partial(_out_proj_kernel, tn=tn),
        grid=(s // tm, d // tn),
        in_specs=[pl.BlockSpec((tm, d), lambda i, j: (i, 0)),
                  pl.BlockSpec((d, tn), lambda i, j: (0, j)),
                  pl.BlockSpec((tm, tn), lambda i, j: (i, j)),
                  pl.BlockSpec((1, d), lambda i, j: (0, 0))],
        out_specs=pl.BlockSpec((tm, d), lambda i, j: (i, 0)),
        out_shape=jax.ShapeDtypeStruct((s, d), F32),
        compiler_params=_params("parallel", "arbitrary"),
        name="out_proj_norm",
    )(merged, w, x, g.reshape(1, d))


def kernel(x, mem, norm_in, norm_mem, w_in, w_pool_group, pool_scale, w_mem_k, w_mem_v,
           w_proj_pool, w_proj_ret, w_proj_mem, w_out, norm_f):
    b, s, d = x.shape
    depth = w_in.shape[0]
    assert b == 1 and d == D_MODEL and depth == 1

    pos = jnp.arange(s, dtype=F32)
    inv = ROPE_BASE ** (-jnp.arange(ROPE_HALF, dtype=F32) / ROPE_HALF)
    ang = pos[:, None] * inv[None, :]
    cos, sin = jnp.cos(ang), jnp.sin(ang)
    lg = jnp.log1p(-(2.0 ** (-5.0 - jnp.arange(RET_HEADS, dtype=F32))))
    lg_tab = jnp.broadcast_to(lg[:, None, None], (RET_HEADS, 8, 128))

    xs = x[0]
    mems = mem[0]
    for l in range(depth):
        h = _rmsnorm(xs, norm_in[l], tm=256)
        z = _in_proj(h, w_in[l].astype(BF16), cos, sin, tm=1024, tn=1024)

        pool_out = _pool_branch(z, w_pool_group[l].astype(BF16), pool_scale[l], tm=512)
        ret_out = _retention_branch(z, lg_tab, chunk=256)

        memn = _rmsnorm(mems, norm_mem[l], tm=256)
        km = _matmul(memn, w_mem_k[l].astype(BF16), tm=256, tn=1024)
        vm = _matmul(memn, w_mem_v[l].astype(BF16), tm=256, tn=1024)
        mem_out = _mem_branch(z, km, vm, tm=1024)

        merged = _gated_proj(pool_out, w_proj_pool[l].astype(BF16), z, SEG_A_POOL, None, 512, 512, F32)
        merged = _gated_proj(ret_out, w_proj_ret[l].astype(BF16), z, SEG_A_RET, merged, 512, 512, F32)
        merged = _gated_proj(mem_out, w_proj_mem[l].astype(BF16), z, SEG_A_MEM, merged, 512, 512, BF16)

        xs = _out_proj(merged, w_out[l].astype(BF16), xs, norm_f, 512, 512)
    return xs[None]
```

```python
import functools

import jax
import jax.numpy as jnp
from jax import lax
from jax.experimental import pallas as pl
from jax.experimental.pallas import tpu as pltpu

D_MODEL = 4096
POOL_WINDOWS = (2, 4, 8, 16)
POOL_GROUPS = 4
POOL_GROUP_DIM = D_MODEL // POOL_GROUPS
POOL_HALO = 16
RET_QK_DIM = 256
RET_HEADS = D_MODEL // RET_QK_DIM
RET_V_DIM = 2 * RET_QK_DIM
RET_V_WIDTH = RET_HEADS * RET_V_DIM
MEM_HEADS = 4
MEM_HEAD_DIM = D_MODEL // MEM_HEADS
ROPE_BASE = 10000.0
ROPE_HALF = RET_QK_DIM // 2
NORM_EPS = 1e-6

SEG_U_POOL, SEG_G_POOL, SEG_Q, SEG_K, SEG_V, SEG_G_RET, SEG_Q_MEM, SEG_G_MEM, SEG_A_POOL, SEG_A_RET, SEG_A_MEM = (
    0, 1, 2, 3, 4, 6, 8, 9, 10, 11, 12)
LIN_SEGS = (SEG_U_POOL, SEG_V, SEG_V + 1, SEG_Q_MEM)
LIN_U_POOL, LIN_V, LIN_Q_MEM = 0, 1, 3
SILU_SEGS = (SEG_G_POOL, SEG_G_RET, SEG_G_RET + 1, SEG_G_MEM)
SILU_G_POOL, SILU_G_RET, SILU_G_MEM = 0, 1, 3
SIG_SEGS = (SEG_A_POOL, SEG_A_RET, SEG_A_MEM)
SIG_A_POOL, SIG_A_RET, SIG_A_MEM = 0, 1, 2
ROPE_SEGS = (SEG_Q, SEG_K)
ROPE_Q, ROPE_K = 0, 1

F32 = jnp.float32
BF16 = jnp.bfloat16
VMEM_LIMIT = 60 * 1024 * 1024


def _params(*sem):
    return pltpu.CompilerParams(dimension_semantics=sem, vmem_limit_bytes=VMEM_LIMIT)


def _sigmoid(x):
    return 1.0 / (1.0 + jnp.exp(-x))


def _rmsnorm_kernel(x_ref, g_ref, o_ref):
    x = x_ref[...]
    y = x * lax.rsqrt(jnp.mean(x * x, axis=-1, keepdims=True) + NORM_EPS)
    o_ref[...] = (y * g_ref[...]).astype(o_ref.dtype)


def _rmsnorm(x, g, tm):
    rows, d = x.shape
    return pl.pallas_call(
        _rmsnorm_kernel,
        grid=(rows // tm,),
        in_specs=[pl.BlockSpec((tm, d), lambda i: (i, 0)),
                  pl.BlockSpec((1, d), lambda i: (0, 0))],
        out_specs=pl.BlockSpec((tm, d), lambda i: (i, 0)),
        out_shape=jax.ShapeDtypeStruct((rows, d), BF16),
        compiler_params=_params("parallel"),
        name="rmsnorm",
    )(x, g.reshape(1, d))


def _in_proj_kernel(h_ref, w_ref, *rest, epilogue, tn):
    o_ref = rest[-1]
    acc = jnp.dot(h_ref[...], w_ref[...].astype(BF16), preferred_element_type=F32)
    if epilogue == "linear":
        o_ref[...] = acc.astype(o_ref.dtype)
    elif epilogue == "silu":
        o_ref[...] = (acc * _sigmoid(acc)).astype(o_ref.dtype)
    elif epilogue == "sigmoid":
        o_ref[...] = _sigmoid(acc).astype(o_ref.dtype)
    else:
        assert epilogue == "rope"
        cos = rest[0][...]
        sin = rest[1][...]
        for hh in range(tn // RET_QK_DIM):
            lo = hh * RET_QK_DIM
            x1 = acc[:, lo:lo + ROPE_HALF]
            x2 = acc[:, lo + ROPE_HALF:lo + RET_QK_DIM]
            o_ref[:, lo:lo + ROPE_HALF] = (x1 * cos - x2 * sin).astype(o_ref.dtype)
            o_ref[:, lo + ROPE_HALF:lo + RET_QK_DIM] = (x2 * cos + x1 * sin).astype(o_ref.dtype)


def _in_proj(h, w, segs, epilogue, cos, sin, tm, tn):
    s, d = h.shape
    nb = D_MODEL // tn

    def w_block(j):
        js = j // nb
        seg = segs[-1]
        for idx in range(len(segs) - 2, -1, -1):
            seg = jnp.where(js == idx, segs[idx], seg)
        return seg * nb + j % nb

    in_specs = [pl.BlockSpec((tm, d), lambda j, i: (i, 0)),
                pl.BlockSpec((d, tn), lambda j, i: (0, w_block(j)))]
    args = [h, w]
    if epilogue == "rope":
        in_specs += [pl.BlockSpec((tm, ROPE_HALF), lambda j, i: (i, 0))] * 2
        args += [cos, sin]
    return pl.pallas_call(
        functools.partial(_in_proj_kernel, epilogue=epilogue, tn=tn),
        grid=(len(segs) * nb, s // tm),
        in_specs=in_specs,
        out_specs=pl.BlockSpec((tm, tn), lambda j, i: (i, j)),
        out_shape=jax.ShapeDtypeStruct((s, len(segs) * D_MODEL), BF16),
        compiler_params=_params("parallel", "arbitrary"),
        name="in_proj_" + epilogue,
    )(*args)


def _matmul_kernel(a_ref, b_ref, o_ref):
    o_ref[...] = jnp.dot(a_ref[...], b_ref[...], preferred_element_type=F32).astype(o_ref.dtype)


def _matmul(a, b, tm, tn):
    m, k = a.shape
    n = b.shape[1]
    return pl.pallas_call(
        _matmul_kernel,
        grid=(m // tm, n // tn),
        in_specs=[pl.BlockSpec((tm, k), lambda i, j: (i, 0)),
                  pl.BlockSpec((k, tn), lambda i, j: (0, j))],
        out_specs=pl.BlockSpec((tm, tn), lambda i, j: (i, j)),
        out_shape=jax.ShapeDtypeStruct((m, n), BF16),
        compiler_params=_params("parallel", "arbitrary"),
        name="mem_kv_proj",
    )(a, b)


def _pool_kernel(u_ref, halo_ref, gate_ref, w_ref, scale_ref, o_ref, buf_ref, *, tm):
    g = pl.program_id(0)
    i = pl.program_id(1)
    u = u_ref[...].astype(F32)
    halo = halo_ref[...].astype(F32)
    buf_ref[0:POOL_HALO, :] = jnp.where(i == 0, 0.0, halo)
    buf_ref[POOL_HALO:POOL_HALO + tm, :] = u
    t = i * tm + lax.broadcasted_iota(jnp.int32, (tm, 1), 0)

    for gi, win in enumerate(POOL_WINDOWS):
        @pl.when(g == gi)
        def _(win=win):
            wsum = u
            for back in range(1, win):
                wsum = wsum + buf_ref[POOL_HALO - back:POOL_HALO - back + tm, :]
            cnt = jnp.minimum(t + 1, win).astype(F32)
            mixed = wsum / cnt - u
            y = jnp.dot(mixed.astype(BF16), w_ref[0], preferred_element_type=F32)
            y = y * scale_ref[...] * gate_ref[...].astype(F32)
            o_ref[...] = y.astype(o_ref.dtype)


def _pool_branch(z_lin, z_silu, w_group, scale, tm):
    s = z_lin.shape[0]
    gd = POOL_GROUP_DIM
    halo_blocks_per_tile = tm // POOL_HALO
    return pl.pallas_call(
        functools.partial(_pool_kernel, tm=tm),
        grid=(POOL_GROUPS, s // tm),
        in_specs=[
            pl.BlockSpec((tm, gd), lambda g, i: (i, LIN_U_POOL * POOL_GROUPS + g)),
            pl.BlockSpec((POOL_HALO, gd),
                         lambda g, i: (jnp.maximum(i * halo_blocks_per_tile - 1, 0), LIN_U_POOL * POOL_GROUPS + g)),
            pl.BlockSpec((tm, gd), lambda g, i: (i, SILU_G_POOL * POOL_GROUPS + g)),
            pl.BlockSpec((1, gd, gd), lambda g, i: (g, 0, 0)),
            pl.BlockSpec((1, gd), lambda g, i: (0, g)),
        ],
        out_specs=pl.BlockSpec((tm, gd), lambda g, i: (i, g)),
        out_shape=jax.ShapeDtypeStruct((s, D_MODEL), BF16),
        scratch_shapes=[pltpu.VMEM((tm + POOL_HALO, gd), F32)],
        compiler_params=_params("parallel", "arbitrary"),
        name="pool_branch",
    )(z_lin, z_lin, z_silu, w_group, scale.reshape(1, D_MODEL))


def _retention_kernel(lg_ref, q_ref, k_ref, v_ref, gate_ref, o_ref, state_ref, *, chunk):
    n = pl.program_id(1)

    @pl.when(n == 0)
    def _():
        state_ref[...] = jnp.zeros_like(state_ref)

    k_scale = RET_QK_DIM ** -0.5
    lg = lg_ref[0][0:1, 0:1]
    row = lax.broadcasted_iota(jnp.int32, (chunk, 1), 0).astype(F32)
    col = lax.broadcasted_iota(jnp.int32, (1, chunk), 1).astype(F32)
    diff = row - col
    intra = jnp.where(diff >= 0, jnp.exp(jnp.maximum(diff, 0.0) * lg) * k_scale, 0.0)
    q_decay = jnp.exp((row + 1.0) * lg)
    k_decay = jnp.exp((chunk - 1.0 - row) * lg) * k_scale
    chunk_decay = jnp.exp(chunk * lg)

    q = q_ref[...]
    k = k_ref[...]
    v = v_ref[...]
    state = state_ref[...]
    scores = lax.dot_general(q, k, (((1,), (1,)), ((), ())), preferred_element_type=F32) * intra
    inner = jnp.dot(scores.astype(BF16), v, preferred_element_type=F32)
    cross = jnp.dot(q, state.astype(BF16), preferred_element_type=F32) * q_decay
    kd = (k.astype(F32) * k_decay).astype(BF16)
    state_ref[...] = state * chunk_decay + lax.dot_general(
        kd, v, (((0,), (0,)), ((), ())), preferred_element_type=F32)

    o = inner + cross
    o = o * lax.rsqrt(jnp.mean(o * o, axis=-1, keepdims=True) + NORM_EPS)
    o_ref[...] = (o * gate_ref[...].astype(F32)).astype(o_ref.dtype)


def _retention_branch(z_qk, z_lin, z_silu, lg_tab, chunk):
    s = z_qk.shape[0]
    q_blk = ROPE_Q * D_MODEL // RET_QK_DIM
    k_blk = ROPE_K * D_MODEL // RET_QK_DIM
    v_blk = LIN_V * D_MODEL // RET_V_DIM
    g_blk = SILU_G_RET * D_MODEL // RET_V_DIM
    return pl.pallas_call(
        functools.partial(_retention_kernel, chunk=chunk),
        grid=(RET_HEADS, s // chunk),
        in_specs=[
            pl.BlockSpec((1, 8, 128), lambda h, n: (h, 0, 0)),
            pl.BlockSpec((chunk, RET_QK_DIM), lambda h, n: (n, q_blk + h)),
            pl.BlockSpec((chunk, RET_QK_DIM), lambda h, n: (n, k_blk + h)),
            pl.BlockSpec((chunk, RET_V_DIM), lambda h, n: (n, v_blk + h)),
            pl.BlockSpec((chunk, RET_V_DIM), lambda h, n: (n, g_blk + h)),
        ],
        out_specs=pl.BlockSpec((chunk, RET_V_DIM), lambda h, n: (n, h)),
        out_shape=jax.ShapeDtypeStruct((s, RET_V_WIDTH), BF16),
        scratch_shapes=[pltpu.VMEM((RET_QK_DIM, RET_V_DIM), F32)],
        compiler_params=_params("parallel", "arbitrary"),
        name="retention_branch",
    )(lg_tab, z_qk, z_qk, z_lin, z_silu)


def _mem_attn_kernel(q_ref, gate_ref, k_ref, v_ref, o_ref):
    scores = lax.dot_general(q_ref[...], k_ref[...], (((1,), (1,)), ((), ())),
                             preferred_element_type=F32) * (MEM_HEAD_DIM ** -0.5)
    e = jnp.exp(scores - jnp.max(scores, axis=-1, keepdims=True))
    p = e * (1.0 / jnp.sum(e, axis=-1, keepdims=True))
    mo = jnp.dot(p.astype(BF16), v_ref[...], preferred_element_type=F32)
    o_ref[...] = (mo * gate_ref[...].astype(F32)).astype(o_ref.dtype)


def _mem_branch(z_lin, z_silu, km, vm, tm):
    s = z_lin.shape[0]
    m = km.shape[0]
    hd = MEM_HEAD_DIM
    return pl.pallas_call(
        _mem_attn_kernel,
        grid=(MEM_HEADS, s // tm),
        in_specs=[
            pl.BlockSpec((tm, hd), lambda h, i: (i, LIN_Q_MEM * MEM_HEADS + h)),
            pl.BlockSpec((tm, hd), lambda h, i: (i, SILU_G_MEM * MEM_HEADS + h)),
            pl.BlockSpec((m, hd), lambda h, i: (0, h)),
            pl.BlockSpec((m, hd), lambda h, i: (0, h)),
        ],
        out_specs=pl.BlockSpec((tm, hd), lambda h, i: (i, h)),
        out_shape=jax.ShapeDtypeStruct((s, D_MODEL), BF16),
        compiler_params=_params("parallel", "arbitrary"),
        name="mem_branch",
    )(z_lin, z_silu, km, vm)


def _gated_proj_kernel(a_ref, w_ref, gate_ref, *rest, has_prev):
    if has_prev:
        prev_ref, o_ref = rest
    else:
        (o_ref,) = rest
    y = jnp.dot(a_ref[...], w_ref[...], preferred_element_type=F32) * gate_ref[...].astype(F32)
    if has_prev:
        y = y + prev_ref[...].astype(F32)
    o_ref[...] = y.astype(o_ref.dtype)


def _gated_proj(a, w, z_sig, gate_pos, prev, tm, tn, out_dtype):
    s, k = a.shape
    n = w.shape[1]
    gate_blk = gate_pos * D_MODEL // tn
    in_specs = [pl.BlockSpec((tm, k), lambda i, j: (i, 0)),
                pl.BlockSpec((k, tn), lambda i, j: (0, j)),
                pl.BlockSpec((tm, tn), lambda i, j: (i, gate_blk + j))]
    args = [a, w, z_sig]
    if prev is not None:
        in_specs.append(pl.BlockSpec((tm, tn), lambda i, j: (i, j)))
        args.append(prev)
    return pl.pallas_call(
        functools.partial(_gated_proj_kernel, has_prev=prev is not None),
        grid=(s // tm, n // tn),
        in_specs=in_specs,
        out_specs=pl.BlockSpec((tm, tn), lambda i, j: (i, j)),
        out_shape=jax.ShapeDtypeStruct((s, n), out_dtype),
        compiler_params=_params("parallel", "arbitrary"),
        name="gated_proj",
    )(*args)


def _out_proj_kernel(m_ref, w_ref, x_ref, g_ref, o_ref, *, tn):
    j = pl.program_id(1)
    col = pl.multiple_of(j * tn, tn)
    o_ref[:, pl.ds(col, tn)] = x_ref[...] + jnp.dot(m_ref[...], w_ref[...], preferred_element_type=F32)

    @pl.when(j == pl.num_programs(1) - 1)
    def _():
        y = o_ref[...]
        y = y * lax.rsqrt(jnp.mean(y * y, axis=-1, keepdims=True) + NORM_EPS)
        o_ref[...] = y * g_ref[...]


def _out_proj(merged, w, x, g, tm, tn):
    s, d = x.shape
    return pl.pallas_call(
        functools.partial(_out_proj_kernel, tn=tn),
        grid=(s // tm, d // tn),
        in_specs=[pl.BlockSpec((tm, d), lambda i, j: (i, 0)),
                  pl.BlockSpec((d, tn), lambda i, j: (0, j)),
                  pl.BlockSpec((tm, tn), lambda i, j: (i, j)),
                  pl.BlockSpec((1, d), lambda i, j: (0, 0))],
        out_specs=pl.BlockSpec((tm, d), lambda i, j: (i, 0)),
        out_shape=jax.ShapeDtypeStruct((s, d), F32),
        compiler_params=_params("parallel", "arbitrary"),
        name="out_proj_norm",
    )(merged, w, x, g.reshape(1, d))


def kernel(x, mem, norm_in, norm_mem, w_in, w_pool_group, pool_scale, w_mem_k, w_mem_v,
           w_proj_pool, w_proj_ret, w_proj_mem, w_out, norm_f):
    b, s, d = x.shape
    depth = w_in.shape[0]
    assert b == 1 and d == D_MODEL and depth == 1

    pos = jnp.arange(s, dtype=F32)
    inv = ROPE_BASE ** (-jnp.arange(ROPE_HALF, dtype=F32) / ROPE_HALF)
    ang = pos[:, None] * inv[None, :]
    cos, sin = jnp.cos(ang), jnp.sin(ang)
    lg = jnp.log1p(-(2.0 ** (-5.0 - jnp.arange(RET_HEADS, dtype=F32))))
    lg_tab = jnp.broadcast_to(lg[:, None, None], (RET_HEADS, 8, 128))

    xs = x[0]
    mems = mem[0]
    h = _rmsnorm(xs, norm_in[0], tm=256)
    tm_in, tn_in = 512, 1024
    z_lin = _in_proj(h, w_in[0], LIN_SEGS, "linear", None, None, tm_in, tn_in)
    z_silu = _in_proj(h, w_in[0], SILU_SEGS, "silu", None, None, tm_in, tn_in)
    z_sig = _in_proj(h, w_in[0], SIG_SEGS, "sigmoid", None, None, tm_in, tn_in)
    z_qk = _in_proj(h, w_in[0], ROPE_SEGS, "rope", cos, sin, tm_in, tn_in)

    pool_out = _pool_branch(z_lin, z_silu, w_pool_group[0].astype(BF16), pool_scale[0], tm=512)
    ret_out = _retention_branch(z_qk, z_lin, z_silu, lg_tab, chunk=256)

    memn = _rmsnorm(mems, norm_mem[0], tm=256)
    km = _matmul(memn, w_mem_k[0].astype(BF16), tm=256, tn=1024)
    vm = _matmul(memn, w_mem_v[0].astype(BF16), tm=256, tn=1024)
    mem_out = _mem_branch(z_lin, z_silu, km, vm, tm=1024)

    merged = _gated_proj(pool_out, w_proj_pool[0].astype(BF16), z_sig, SIG_A_POOL, None, 512, 512, F32)
    merged = _gated_proj(ret_out, w_proj_ret[0].astype(BF16), z_sig, SIG_A_RET, merged, 512, 512, F32)
    merged = _gated_proj(mem_out, w_proj_mem[0].astype(BF16), z_sig, SIG_A_MEM, merged, 512, 512, BF16)

    out = _out_proj(merged, w_out[0].astype(BF16), xs, norm_f, 512, 512)
    return out[None]
```

```python
import functools

import jax
import jax.numpy as jnp
from jax import lax
from jax.experimental import pallas as pl
from jax.experimental.pallas import tpu as pltpu

D_MODEL = 4096
POOL_WINDOWS = (2, 4, 8, 16)
POOL_GROUPS = 4
POOL_GROUP_DIM = D_MODEL // POOL_GROUPS
POOL_HALO = 16
RET_QK_DIM = 256
RET_HEADS = D_MODEL // RET_QK_DIM
RET_V_DIM = 2 * RET_QK_DIM
RET_V_WIDTH = RET_HEADS * RET_V_DIM
MEM_HEADS = 4
MEM_HEAD_DIM = D_MODEL // MEM_HEADS
ROPE_BASE = 10000.0
ROPE_HALF = RET_QK_DIM // 2
NORM_EPS = 1e-6

SEG_U_POOL, SEG_G_POOL, SEG_Q, SEG_K, SEG_V, SEG_G_RET, SEG_Q_MEM, SEG_G_MEM, SEG_A_POOL, SEG_A_RET, SEG_A_MEM = (
    0, 1, 2, 3, 4, 6, 8, 9, 10, 11, 12)
LIN_SEGS = (SEG_U_POOL, SEG_V, SEG_V + 1, SEG_Q_MEM)
LIN_U_POOL, LIN_V, LIN_Q_MEM = 0, 1, 3
SILU_SEGS = (SEG_G_POOL, SEG_G_RET, SEG_G_RET + 1, SEG_G_MEM)
SILU_G_POOL, SILU_G_RET, SILU_G_MEM = 0, 1, 3
SIG_SEGS = (SEG_A_POOL, SEG_A_RET, SEG_A_MEM)
SIG_A_POOL, SIG_A_RET, SIG_A_MEM = 0, 1, 2
ROPE_SEGS = (SEG_Q, SEG_K)
ROPE_Q, ROPE_K = 0, 1

F32 = jnp.float32
BF16 = jnp.bfloat16
VMEM_LIMIT = 60 * 1024 * 1024


def _params(*sem):
    return pltpu.CompilerParams(dimension_semantics=sem, vmem_limit_bytes=VMEM_LIMIT)


def _sigmoid(x):
    return 1.0 / (1.0 + jnp.exp(-x))


def _rmsnorm_kernel(x_ref, g_ref, o_ref):
    x = x_ref[...]
    y = x * lax.rsqrt(jnp.mean(x * x, axis=-1, keepdims=True) + NORM_EPS)
    o_ref[...] = (y * g_ref[...]).astype(o_ref.dtype)


def _rmsnorm(x, g, tm):
    rows, d = x.shape
    return pl.pallas_call(
        _rmsnorm_kernel,
        grid=(rows // tm,),
        in_specs=[pl.BlockSpec((tm, d), lambda i: (i, 0)),
                  pl.BlockSpec((1, d), lambda i: (0, 0))],
        out_specs=pl.BlockSpec((tm, d), lambda i: (i, 0)),
        out_shape=jax.ShapeDtypeStruct((rows, d), BF16),
        compiler_params=_params("parallel"),
        name="rmsnorm",
    )(x, g.reshape(1, d))


def _in_proj_kernel(h_ref, w_ref, *rest, epilogue, tn):
    o_ref = rest[-1]
    acc = jnp.dot(h_ref[...], w_ref[...].astype(BF16), preferred_element_type=F32)
    if epilogue == "linear":
        o_ref[...] = acc.astype(o_ref.dtype)
    elif epilogue == "silu":
        o_ref[...] = (acc * _sigmoid(acc)).astype(o_ref.dtype)
    elif epilogue == "sigmoid":
        o_ref[...] = _sigmoid(acc).astype(o_ref.dtype)
    else:
        assert epilogue == "rope"
        cos = rest[0][...]
        sin = rest[1][...]
        for hh in range(tn // RET_QK_DIM):
            lo = hh * RET_QK_DIM
            x1 = acc[:, lo:lo + ROPE_HALF]
            x2 = acc[:, lo + ROPE_HALF:lo + RET_QK_DIM]
            o_ref[:, lo:lo + ROPE_HALF] = (x1 * cos - x2 * sin).astype(o_ref.dtype)
            o_ref[:, lo + ROPE_HALF:lo + RET_QK_DIM] = (x2 * cos + x1 * sin).astype(o_ref.dtype)


def _in_proj(h, w, segs, epilogue, cos, sin, tm, tn):
    s, d = h.shape
    nb = D_MODEL // tn

    def w_block(j):
        js = j // nb
        seg = segs[-1]
        for idx in range(len(segs) - 2, -1, -1):
            seg = jnp.where(js == idx, segs[idx], seg)
        return seg * nb + j % nb

    in_specs = [pl.BlockSpec((tm, d), lambda i, j: (i, 0), pipeline_mode=pl.Buffered(1)),
                pl.BlockSpec((d, tn), lambda i, j: (0, w_block(j)))]
    args = [h, w]
    if epilogue == "rope":
        in_specs += [pl.BlockSpec((tm, ROPE_HALF), lambda i, j: (i, 0))] * 2
        args += [cos, sin]
    return pl.pallas_call(
        functools.partial(_in_proj_kernel, epilogue=epilogue, tn=tn),
        grid=(s // tm, len(segs) * nb),
        in_specs=in_specs,
        out_specs=pl.BlockSpec((tm, tn), lambda i, j: (i, j)),
        out_shape=jax.ShapeDtypeStruct((s, len(segs) * D_MODEL), BF16),
        compiler_params=_params("parallel", "arbitrary"),
        name="in_proj_" + epilogue,
    )(*args)


def _matmul_kernel(a_ref, b_ref, o_ref):
    o_ref[...] = jnp.dot(a_ref[...], b_ref[...], preferred_element_type=F32).astype(o_ref.dtype)


def _matmul(a, b, tm, tn):
    m, k = a.shape
    n = b.shape[1]
    return pl.pallas_call(
        _matmul_kernel,
        grid=(m // tm, n // tn),
        in_specs=[pl.BlockSpec((tm, k), lambda i, j: (i, 0)),
                  pl.BlockSpec((k, tn), lambda i, j: (0, j))],
        out_specs=pl.BlockSpec((tm, tn), lambda i, j: (i, j)),
        out_shape=jax.ShapeDtypeStruct((m, n), BF16),
        compiler_params=_params("parallel", "arbitrary"),
        name="mem_kv_proj",
    )(a, b)


def _pool_kernel(u_ref, halo_ref, gate_ref, w_ref, scale_ref, o_ref, buf_ref, *, tm):
    g = pl.program_id(0)
    i = pl.program_id(1)
    u = u_ref[...].astype(F32)
    halo = halo_ref[...].astype(F32)
    buf_ref[0:POOL_HALO, :] = jnp.where(i == 0, 0.0, halo)
    buf_ref[POOL_HALO:POOL_HALO + tm, :] = u
    t = i * tm + lax.broadcasted_iota(jnp.int32, (tm, 1), 0)

    for gi, win in enumerate(POOL_WINDOWS):
        @pl.when(g == gi)
        def _(win=win):
            wsum = u
            for back in range(1, win):
                wsum = wsum + buf_ref[POOL_HALO - back:POOL_HALO - back + tm, :]
            cnt = jnp.minimum(t + 1, win).astype(F32)
            mixed = wsum / cnt - u
            y = jnp.dot(mixed.astype(BF16), w_ref[0], preferred_element_type=F32)
            y = y * scale_ref[...] * gate_ref[...].astype(F32)
            o_ref[...] = y.astype(o_ref.dtype)


def _pool_branch(z_lin, z_silu, w_group, scale, tm):
    s = z_lin.shape[0]
    gd = POOL_GROUP_DIM
    halo_blocks_per_tile = tm // POOL_HALO
    return pl.pallas_call(
        functools.partial(_pool_kernel, tm=tm),
        grid=(POOL_GROUPS, s // tm),
        in_specs=[
            pl.BlockSpec((tm, gd), lambda g, i: (i, LIN_U_POOL * POOL_GROUPS + g)),
            pl.BlockSpec((POOL_HALO, gd),
                         lambda g, i: (jnp.maximum(i * halo_blocks_per_tile - 1, 0), LIN_U_POOL * POOL_GROUPS + g)),
            pl.BlockSpec((tm, gd), lambda g, i: (i, SILU_G_POOL * POOL_GROUPS + g)),
            pl.BlockSpec((1, gd, gd), lambda g, i: (g, 0, 0)),
            pl.BlockSpec((1, gd), lambda g, i: (0, g)),
        ],
        out_specs=pl.BlockSpec((tm, gd), lambda g, i: (i, g)),
        out_shape=jax.ShapeDtypeStruct((s, D_MODEL), BF16),
        scratch_shapes=[pltpu.VMEM((tm + POOL_HALO, gd), F32)],
        compiler_params=_params("parallel", "arbitrary"),
        name="pool_branch",
    )(z_lin, z_lin, z_silu, w_group, scale.reshape(1, D_MODEL))


def _retention_kernel(lg_ref, q_ref, k_ref, v_ref, gate_ref, o_ref, state_ref, *, chunk, n_chunks):
    @pl.when(pl.program_id(1) == 0)
    def _():
        state_ref[...] = jnp.zeros_like(state_ref)

    k_scale = RET_QK_DIM ** -0.5
    lg = lg_ref[0][0:1, 0:1]
    row = lax.broadcasted_iota(jnp.int32, (chunk, 1), 0).astype(F32)
    col = lax.broadcasted_iota(jnp.int32, (1, chunk), 1).astype(F32)
    diff = row - col
    intra = jnp.where(diff >= 0, jnp.exp(jnp.maximum(diff, 0.0) * lg) * k_scale, 0.0)
    q_decay = jnp.exp((row + 1.0) * lg)
    k_decay = jnp.exp((chunk - 1.0 - row) * lg) * k_scale
    chunk_decay = jnp.exp(chunk * lg)

    for c in range(n_chunks):
        rows = slice(c * chunk, (c + 1) * chunk)
        q = q_ref[rows, :]
        k = k_ref[rows, :]
        v = v_ref[rows, :]
        state = state_ref[...]
        scores = lax.dot_general(q, k, (((1,), (1,)), ((), ())), preferred_element_type=F32) * intra
        inner = jnp.dot(scores.astype(BF16), v, preferred_element_type=F32)
        cross = jnp.dot(q, state.astype(BF16), preferred_element_type=F32) * q_decay
        kd = (k.astype(F32) * k_decay).astype(BF16)
        state_ref[...] = state * chunk_decay + lax.dot_general(
            kd, v, (((0,), (0,)), ((), ())), preferred_element_type=F32)

        o = inner + cross
        o = o * lax.rsqrt(jnp.mean(o * o, axis=-1, keepdims=True) + NORM_EPS)
        o_ref[rows, :] = (o * gate_ref[rows, :].astype(F32)).astype(o_ref.dtype)


def _retention_branch(z_qk, z_lin, z_silu, lg_tab, chunk, n_chunks):
    s = z_qk.shape[0]
    rows = chunk * n_chunks
    q_blk = ROPE_Q * D_MODEL // RET_QK_DIM
    k_blk = ROPE_K * D_MODEL // RET_QK_DIM
    v_blk = LIN_V * D_MODEL // RET_V_DIM
    g_blk = SILU_G_RET * D_MODEL // RET_V_DIM
    return pl.pallas_call(
        functools.partial(_retention_kernel, chunk=chunk, n_chunks=n_chunks),
        grid=(RET_HEADS, s // rows),
        in_specs=[
            pl.BlockSpec((1, 8, 128), lambda h, n: (h, 0, 0)),
            pl.BlockSpec((rows, RET_QK_DIM), lambda h, n: (n, q_blk + h)),
            pl.BlockSpec((rows, RET_QK_DIM), lambda h, n: (n, k_blk + h)),
            pl.BlockSpec((rows, RET_V_DIM), lambda h, n: (n, v_blk + h)),
            pl.BlockSpec((rows, RET_V_DIM), lambda h, n: (n, g_blk + h)),
        ],
        out_specs=pl.BlockSpec((rows, RET_V_DIM), lambda h, n: (n, h)),
        out_shape=jax.ShapeDtypeStruct((s, RET_V_WIDTH), BF16),
        scratch_shapes=[pltpu.VMEM((RET_QK_DIM, RET_V_DIM), F32)],
        compiler_params=_params("parallel", "arbitrary"),
        name="retention_branch",
    )(lg_tab, z_qk, z_qk, z_lin, z_silu)


def _mem_attn_kernel(q_ref, gate_ref, k_ref, v_ref, o_ref):
    scores = lax.dot_general(q_ref[...], k_ref[...], (((1,), (1,)), ((), ())),
                             preferred_element_type=F32) * (MEM_HEAD_DIM ** -0.5)
    e = jnp.exp(scores - jnp.max(scores, axis=-1, keepdims=True))
    p = e * (1.0 / jnp.sum(e, axis=-1, keepdims=True))
    mo = jnp.dot(p.astype(BF16), v_ref[...], preferred_element_type=F32)
    o_ref[...] = (mo * gate_ref[...].astype(F32)).astype(o_ref.dtype)


def _mem_branch(z_lin, z_silu, km, vm, tm):
    s = z_lin.shape[0]
    m = km.shape[0]
    hd = MEM_HEAD_DIM
    return pl.pallas_call(
        _mem_attn_kernel,
        grid=(MEM_HEADS, s // tm),
        in_specs=[
            pl.BlockSpec((tm, hd), lambda h, i: (i, LIN_Q_MEM * MEM_HEADS + h)),
            pl.BlockSpec((tm, hd), lambda h, i: (i, SILU_G_MEM * MEM_HEADS + h)),
            pl.BlockSpec((m, hd), lambda h, i: (0, h)),
            pl.BlockSpec((m, hd), lambda h, i: (0, h)),
        ],
        out_specs=pl.BlockSpec((tm, hd), lambda h, i: (i, h)),
        out_shape=jax.ShapeDtypeStruct((s, D_MODEL), BF16),
        compiler_params=_params("parallel", "arbitrary"),
        name="mem_branch",
    )(z_lin, z_silu, km, vm)


def _gated_proj_kernel(a_ref, w_ref, gate_ref, *rest, has_prev):
    if has_prev:
        prev_ref, o_ref = rest
    else:
        (o_ref,) = rest
    y = jnp.dot(a_ref[...], w_ref[...], preferred_element_type=F32) * gate_ref[...].astype(F32)
    if has_prev:
        y = y + prev_ref[...].astype(F32)
    o_ref[...] = y.astype(o_ref.dtype)


def _gated_proj(a, w, z_sig, gate_pos, prev, tm, tn, out_dtype):
    s, k = a.shape
    n = w.shape[1]
    gate_blk = gate_pos * D_MODEL // tn
    in_specs = [pl.BlockSpec((tm, k), lambda i, j: (i, 0), pipeline_mode=pl.Buffered(1)),
                pl.BlockSpec((k, tn), lambda i, j: (0, j)),
                pl.BlockSpec((tm, tn), lambda i, j: (i, gate_blk + j))]
    args = [a, w, z_sig]
    if prev is not None:
        in_specs.append(pl.BlockSpec((tm, tn), lambda i, j: (i, j)))
        args.append(prev)
    return pl.pallas_call(
        functools.partial(_gated_proj_kernel, has_prev=prev is not None),
        grid=(s // tm, n // tn),
        in_specs=in_specs,
        out_specs=pl.BlockSpec((tm, tn), lambda i, j: (i, j)),
        out_shape=jax.ShapeDtypeStruct((s, n), out_dtype),
        compiler_params=_params("parallel", "arbitrary"),
        name="gated_proj",
    )(*args)


def _out_proj_kernel(m_ref, w_ref, x_ref, g_ref, o_ref, *, tn):
    j = pl.program_id(1)
    col = pl.multiple_of(j * tn, tn)
    o_ref[:, pl.ds(col, tn)] = x_ref[...] + jnp.dot(m_ref[...], w_ref[...], preferred_element_type=F32)

    @pl.when(j == pl.num_programs(1) - 1)
    def _():
        y = o_ref[...]
        y = y * lax.rsqrt(jnp.mean(y * y, axis=-1, keepdims=True) + NORM_EPS)
        o_ref[...] = y * g_ref[...]


def _out_proj(merged, w, x, g, tm, tn):
    s, d = x.shape
    return pl.pallas_call(
        functools.partial(_out_proj_kernel, tn=tn),
        grid=(s // tm, d // tn),
        in_specs=[pl.BlockSpec((tm, d), lambda i, j: (i, 0)),
                  pl.BlockSpec((d, tn), lambda i, j: (0, j)),
                  pl.BlockSpec((tm, tn), lambda i, j: (i, j)),
                  pl.BlockSpec((1, d), lambda i, j: (0, 0))],
        out_specs=pl.BlockSpec((tm, d), lambda i, j: (i, 0)),
        out_shape=jax.ShapeDtypeStruct((s, d), F32),
        compiler_params=_params("parallel", "arbitrary"),
        name="out_proj_norm",
    )(merged, w, x, g.reshape(1, d))


def kernel(x, mem, norm_in, norm_mem, w_in, w_pool_group, pool_scale, w_mem_k, w_mem_v,
           w_proj_pool, w_proj_ret, w_proj_mem, w_out, norm_f):
    b, s, d = x.shape
    depth = w_in.shape[0]
    assert b == 1 and d == D_MODEL and depth == 1

    pos = jnp.arange(s, dtype=F32)
    inv = ROPE_BASE ** (-jnp.arange(ROPE_HALF, dtype=F32) / ROPE_HALF)
    ang = pos[:, None] * inv[None, :]
    cos, sin = jnp.cos(ang), jnp.sin(ang)
    lg = jnp.log1p(-(2.0 ** (-5.0 - jnp.arange(RET_HEADS, dtype=F32))))
    lg_tab = jnp.broadcast_to(lg[:, None, None], (RET_HEADS, 8, 128))

    xs = x[0]
    mems = mem[0]
    h = _rmsnorm(xs, norm_in[0], tm=256)
    tm_in, tn_in = 2048, 512
    z_lin = _in_proj(h, w_in[0], LIN_SEGS, "linear", None, None, tm_in, tn_in)
    z_silu = _in_proj(h, w_in[0], SILU_SEGS, "silu", None, None, tm_in, tn_in)
    z_sig = _in_proj(h, w_in[0], SIG_SEGS, "sigmoid", None, None, tm_in, tn_in)
    z_qk = _in_proj(h, w_in[0], ROPE_SEGS, "rope", cos, sin, tm_in, tn_in)

    pool_out = _pool_branch(z_lin, z_silu, w_pool_group[0].astype(BF16), pool_scale[0], tm=512)
    ret_out = _retention_branch(z_qk, z_lin, z_silu, lg_tab, chunk=256, n_chunks=4)

    memn = _rmsnorm(mems, norm_mem[0], tm=256)
    km = _matmul(memn, w_mem_k[0].astype(BF16), tm=256, tn=1024)
    vm = _matmul(memn, w_mem_v[0].astype(BF16), tm=256, tn=1024)
    mem_out = _mem_branch(z_lin, z_silu, km, vm, tm=1024)

    merged = _gated_proj(pool_out, w_proj_pool[0].astype(BF16), z_sig, SIG_A_POOL, None, 1024, 1024, F32)
    merged = _gated_proj(ret_out, w_proj_ret[0].astype(BF16), z_sig, SIG_A_RET, merged, 1024, 512, F32)
    merged = _gated_proj(mem_out, w_proj_mem[0].astype(BF16), z_sig, SIG_A_MEM, merged, 1024, 1024, BF16)

    out = _out_proj(merged, w_out[0].astype(BF16), xs, norm_f, 512, 1024)
    return out[None]
```

```python
import functools

import jax
import jax.numpy as jnp
from jax import lax
from jax.experimental import pallas as pl
from jax.experimental.pallas import tpu as pltpu

D_MODEL = 4096
POOL_WINDOWS = (2, 4, 8, 16)
POOL_GROUPS = 4
POOL_GROUP_DIM = D_MODEL // POOL_GROUPS
POOL_HALO = 32
RET_QK_DIM = 256
RET_HEADS = D_MODEL // RET_QK_DIM
RET_V_DIM = 2 * RET_QK_DIM
RET_V_WIDTH = RET_HEADS * RET_V_DIM
MEM_HEADS = 4
MEM_HEAD_DIM = D_MODEL // MEM_HEADS
ROPE_BASE = 10000.0
ROPE_HALF = RET_QK_DIM // 2
NORM_EPS = 1e-6

SEG_U_POOL, SEG_G_POOL, SEG_Q, SEG_K, SEG_V, SEG_G_RET, SEG_Q_MEM, SEG_G_MEM, SEG_A_POOL, SEG_A_RET, SEG_A_MEM = (
    0, 1, 2, 3, 4, 6, 8, 9, 10, 11, 12)
LIN_SEGS = (SEG_U_POOL, SEG_V, SEG_V + 1, SEG_Q_MEM)
LIN_U_POOL, LIN_V, LIN_Q_MEM = 0, 1, 3
SILU_SEGS = (SEG_G_POOL, SEG_G_RET, SEG_G_RET + 1, SEG_G_MEM)
SILU_G_POOL, SILU_G_RET, SILU_G_MEM = 0, 1, 3
SIG_SEGS = (SEG_A_POOL, SEG_A_RET, SEG_A_MEM)
SIG_A_POOL, SIG_A_RET, SIG_A_MEM = 0, 1, 2
ROPE_SEGS = (SEG_Q, SEG_K)
ROPE_Q, ROPE_K = 0, 1

F32 = jnp.float32
BF16 = jnp.bfloat16
VMEM_LIMIT = 60 * 1024 * 1024


def _params(*sem):
    return pltpu.CompilerParams(dimension_semantics=sem, vmem_limit_bytes=VMEM_LIMIT)


def _sigmoid(x):
    return 0.5 * jnp.tanh(0.5 * x) + 0.5


def _rmsnorm_kernel(x_ref, g_ref, o_ref):
    x = x_ref[...]
    y = x * lax.rsqrt(jnp.mean(x * x, axis=-1, keepdims=True) + NORM_EPS)
    o_ref[...] = (y * g_ref[...]).astype(o_ref.dtype)


def _rmsnorm(x, g, tm):
    rows, d = x.shape
    return pl.pallas_call(
        _rmsnorm_kernel,
        grid=(rows // tm,),
        in_specs=[pl.BlockSpec((tm, d), lambda i: (i, 0)),
                  pl.BlockSpec((1, d), lambda i: (0, 0))],
        out_specs=pl.BlockSpec((tm, d), lambda i: (i, 0)),
        out_shape=jax.ShapeDtypeStruct((rows, d), BF16),
        compiler_params=_params("parallel"),
        name="rmsnorm",
    )(x, g.reshape(1, d))


def _in_proj_kernel(*refs, epilogue, tn, sub_rows, n_casts):
    h_ref, w_ref = refs[:2]
    n_in = 2 + (2 if epilogue == "rope" else 0)
    cast_in = refs[n_in:n_in + n_casts]
    o_ref = refs[n_in + n_casts]
    cast_out = refs[n_in + n_casts + 1:]
    w = w_ref[...].astype(BF16)
    for r in range(h_ref.shape[0] // sub_rows):
        rows = slice(r * sub_rows, (r + 1) * sub_rows)
        acc = jnp.dot(h_ref[rows, :], w, preferred_element_type=F32)
        if epilogue == "linear":
            o_ref[rows, :] = acc.astype(o_ref.dtype)
        elif epilogue == "silu":
            o_ref[rows, :] = (acc * _sigmoid(acc)).astype(o_ref.dtype)
        elif epilogue == "sigmoid":
            o_ref[rows, :] = _sigmoid(acc).astype(o_ref.dtype)
        else:
            assert epilogue == "rope"
            cos = refs[2][rows, :]
            sin = refs[3][rows, :]
            for hh in range(tn // RET_QK_DIM):
                lo = hh * RET_QK_DIM
                x1 = acc[:, lo:lo + ROPE_HALF]
                x2 = acc[:, lo + ROPE_HALF:lo + RET_QK_DIM]
                o_ref[rows, lo:lo + ROPE_HALF] = (x1 * cos - x2 * sin).astype(o_ref.dtype)
                o_ref[rows, lo + ROPE_HALF:lo + RET_QK_DIM] = (x2 * cos + x1 * sin).astype(o_ref.dtype)
    for src_ref, dst_ref in zip(cast_in, cast_out):
        dst_ref[...] = src_ref[...].astype(dst_ref.dtype)


def _in_proj(h, w, segs, epilogue, cos, sin, tm, tn, casts=()):
    s, d = h.shape
    nb = D_MODEL // tn
    nj = len(segs) * nb
    steps = (s // tm) * nj

    def w_block(j):
        js = j // nb
        seg = segs[-1]
        for idx in range(len(segs) - 2, -1, -1):
            seg = jnp.where(js == idx, segs[idx], seg)
        return seg * nb + j % nb

    in_specs = [pl.BlockSpec((tm, d), lambda i, j: (i, 0), pipeline_mode=pl.Buffered(1)),
                pl.BlockSpec((d, tn), lambda i, j: (0, w_block(j)))]
    args = [h, w]
    if epilogue == "rope":
        in_specs += [pl.BlockSpec((tm, ROPE_HALF), lambda i, j: (i, 0))] * 2
        args += [cos, sin]
    out_specs = [pl.BlockSpec((tm, tn), lambda i, j: (i, j))]
    out_shape = [jax.ShapeDtypeStruct((s, len(segs) * D_MODEL), BF16)]
    for c in casts:
        slab = pl.BlockSpec((c.shape[0] // steps, c.shape[1]), lambda i, j: (i * nj + j, 0))
        in_specs.append(slab)
        out_specs.append(slab)
        out_shape.append(jax.ShapeDtypeStruct(c.shape, BF16))
        args.append(c)
    return pl.pallas_call(
        functools.partial(_in_proj_kernel, epilogue=epilogue, tn=tn, sub_rows=256, n_casts=len(casts)),
        grid=(s // tm, nj),
        in_specs=in_specs,
        out_specs=out_specs,
        out_shape=out_shape,
        compiler_params=_params("parallel", "arbitrary"),
        name="in_proj_" + epilogue,
    )(*args)


def _matmul_kernel(a_ref, b_ref, o_ref):
    o_ref[...] = jnp.dot(a_ref[...], b_ref[...], preferred_element_type=F32).astype(o_ref.dtype)


def _matmul(a, b, tm, tn):
    m, k = a.shape
    n = b.shape[1]
    return pl.pallas_call(
        _matmul_kernel,
        grid=(m // tm, n // tn),
        in_specs=[pl.BlockSpec((tm, k), lambda i, j: (i, 0)),
                  pl.BlockSpec((k, tn), lambda i, j: (0, j))],
        out_specs=pl.BlockSpec((tm, tn), lambda i, j: (i, j)),
        out_shape=jax.ShapeDtypeStruct((m, n), BF16),
        compiler_params=_params("parallel", "arbitrary"),
        name="mem_kv_proj",
    )(a, b)


def _pool_kernel(u_ref, halo_ref, gate_ref, w_ref, scale_ref, o_ref, buf_a, buf_b, *, tm):
    g = pl.program_id(0)
    i = pl.program_id(1)
    u = u_ref[...].astype(F32)
    halo = halo_ref[...].astype(F32)
    end = POOL_HALO + tm
    buf_a[0:POOL_HALO, :] = jnp.where(i == 0, 0.0, halo)
    buf_a[POOL_HALO:end, :] = u
    t = i * tm + lax.broadcasted_iota(jnp.int32, (tm, 1), 0)

    for gi, win in enumerate(POOL_WINDOWS):
        @pl.when(g == gi)
        def _(win=win):
            src, dst = buf_a, buf_b
            span, lo = 1, 8
            while span < win:
                dst[lo:end, :] = src[lo:end, :] + src[lo - span:end - span, :]
                src, dst = dst, src
                span, lo = 2 * span, lo + 8
            wsum = src[POOL_HALO:end, :]
            cnt = jnp.minimum(t + 1, win).astype(F32)
            mixed = wsum / cnt - u
            y = jnp.dot(mixed.astype(BF16), w_ref[0], preferred_element_type=F32)
            y = y * scale_ref[...] * gate_ref[...].astype(F32)
            o_ref[...] = y.astype(o_ref.dtype)


def _pool_branch(z_lin, z_silu, w_group, scale, tm):
    s = z_lin.shape[0]
    gd = POOL_GROUP_DIM
    halo_blocks_per_tile = tm // POOL_HALO
    return pl.pallas_call(
        functools.partial(_pool_kernel, tm=tm),
        grid=(POOL_GROUPS, s // tm),
        in_specs=[
            pl.BlockSpec((tm, gd), lambda g, i: (i, LIN_U_POOL * POOL_GROUPS + g)),
            pl.BlockSpec((POOL_HALO, gd),
                         lambda g, i: (jnp.maximum(i * halo_blocks_per_tile - 1, 0), LIN_U_POOL * POOL_GROUPS + g)),
            pl.BlockSpec((tm, gd), lambda g, i: (i, SILU_G_POOL * POOL_GROUPS + g)),
            pl.BlockSpec((1, gd, gd), lambda g, i: (g, 0, 0)),
            pl.BlockSpec((1, gd), lambda g, i: (0, g)),
        ],
        out_specs=pl.BlockSpec((tm, gd), lambda g, i: (i, g)),
        out_shape=jax.ShapeDtypeStruct((s, D_MODEL), BF16),
        scratch_shapes=[pltpu.VMEM((tm + POOL_HALO, gd), F32)] * 2,
        compiler_params=_params("parallel", "arbitrary"),
        name="pool_branch",
    )(z_lin, z_lin, z_silu, w_group, scale.reshape(1, D_MODEL))


def _retention_kernel(lg_ref, q_ref, k_ref, v_ref, gate_ref, o_ref, state_ref, *, chunk, n_chunks):
    @pl.when(pl.program_id(1) == 0)
    def _():
        state_ref[...] = jnp.zeros_like(state_ref)

    k_scale = RET_QK_DIM ** -0.5
    lg = lg_ref[0][0:1, 0:1]
    row = lax.broadcasted_iota(jnp.int32, (chunk, 1), 0).astype(F32)
    col = lax.broadcasted_iota(jnp.int32, (1, chunk), 1).astype(F32)
    diff = row - col
    intra = jnp.where(diff >= 0, jnp.exp(jnp.maximum(diff, 0.0) * lg) * k_scale, 0.0)
    q_decay = jnp.exp((row + 1.0) * lg)
    k_decay = jnp.exp((chunk - 1.0 - row) * lg) * k_scale
    chunk_decay = jnp.exp(chunk * lg)

    for c in range(n_chunks):
        rows = slice(c * chunk, (c + 1) * chunk)
        q = q_ref[rows, :]
        k = k_ref[rows, :]
        v = v_ref[rows, :]
        state = state_ref[...]
        scores = lax.dot_general(q, k, (((1,), (1,)), ((), ())), preferred_element_type=F32) * intra
        inner = jnp.dot(scores.astype(BF16), v, preferred_element_type=F32)
        cross = jnp.dot(q, state.astype(BF16), preferred_element_type=F32) * q_decay
        kd = (k.astype(F32) * k_decay).astype(BF16)
        state_ref[...] = state * chunk_decay + lax.dot_general(
            kd, v, (((0,), (0,)), ((), ())), preferred_element_type=F32)

        o = inner + cross
        o = o * lax.rsqrt(jnp.mean(o * o, axis=-1, keepdims=True) + NORM_EPS)
        o_ref[rows, :] = (o * gate_ref[rows, :].astype(F32)).astype(o_ref.dtype)


def _retention_branch(z_qk, z_lin, z_silu, lg_tab, chunk, n_chunks):
    s = z_qk.shape[0]
    rows = chunk * n_chunks
    q_blk = ROPE_Q * D_MODEL // RET_QK_DIM
    k_blk = ROPE_K * D_MODEL // RET_QK_DIM
    v_blk = LIN_V * D_MODEL // RET_V_DIM
    g_blk = SILU_G_RET * D_MODEL // RET_V_DIM
    return pl.pallas_call(
        functools.partial(_retention_kernel, chunk=chunk, n_chunks=n_chunks),
        grid=(RET_HEADS, s // rows),
        in_specs=[
            pl.BlockSpec((1, 8, 128), lambda h, n: (h, 0, 0)),
            pl.BlockSpec((rows, RET_QK_DIM), lambda h, n: (n, q_blk + h)),
            pl.BlockSpec((rows, RET_QK_DIM), lambda h, n: (n, k_blk + h)),
            pl.BlockSpec((rows, RET_V_DIM), lambda h, n: (n, v_blk + h)),
            pl.BlockSpec((rows, RET_V_DIM), lambda h, n: (n, g_blk + h)),
        ],
        out_specs=pl.BlockSpec((rows, RET_V_DIM), lambda h, n: (n, h)),
        out_shape=jax.ShapeDtypeStruct((s, RET_V_WIDTH), BF16),
        scratch_shapes=[pltpu.VMEM((RET_QK_DIM, RET_V_DIM), F32)],
        compiler_params=_params("parallel", "arbitrary"),
        name="retention_branch",
    )(lg_tab, z_qk, z_qk, z_lin, z_silu)


def _mem_attn_kernel(q_ref, gate_ref, k_ref, v_ref, o_ref):
    scores = lax.dot_general(q_ref[...], k_ref[...], (((1,), (1,)), ((), ())),
                             preferred_element_type=F32) * (MEM_HEAD_DIM ** -0.5)
    e = jnp.exp(scores - jnp.max(scores, axis=-1, keepdims=True))
    p = e * (1.0 / jnp.sum(e, axis=-1, keepdims=True))
    mo = jnp.dot(p.astype(BF16), v_ref[...], preferred_element_type=F32)
    o_ref[...] = (mo * gate_ref[...].astype(F32)).astype(o_ref.dtype)


def _mem_branch(z_lin, z_silu, km, vm, tm):
    s = z_lin.shape[0]
    m = km.shape[0]
    hd = MEM_HEAD_DIM
    return pl.pallas_call(
        _mem_attn_kernel,
        grid=(MEM_HEADS, s // tm),
        in_specs=[
            pl.BlockSpec((tm, hd), lambda h, i: (i, LIN_Q_MEM * MEM_HEADS + h)),
            pl.BlockSpec((tm, hd), lambda h, i: (i, SILU_G_MEM * MEM_HEADS + h)),
            pl.BlockSpec((m, hd), lambda h, i: (0, h)),
            pl.BlockSpec((m, hd), lambda h, i: (0, h)),
        ],
        out_specs=pl.BlockSpec((tm, hd), lambda h, i: (i, h)),
        out_shape=jax.ShapeDtypeStruct((s, D_MODEL), BF16),
        compiler_params=_params("parallel", "arbitrary"),
        name="mem_branch",
    )(z_lin, z_silu, km, vm)


def _gated_proj_kernel(a_ref, w_ref, gate_ref, *rest, has_prev):
    if has_prev:
        prev_ref, o_ref = rest
    else:
        (o_ref,) = rest
    y = jnp.dot(a_ref[...], w_ref[...], preferred_element_type=F32) * gate_ref[...].astype(F32)
    if has_prev:
        y = y + prev_ref[...].astype(F32)
    o_ref[...] = y.astype(o_ref.dtype)


def _gated_proj(a, w, z_sig, gate_pos, prev, tm, tn, out_dtype):
    s, k = a.shape
    n = w.shape[1]
    gate_blk = gate_pos * D_MODEL // tn
    in_specs = [pl.BlockSpec((tm, k), lambda i, j: (i, 0), pipeline_mode=pl.Buffered(1)),
                pl.BlockSpec((k, tn), lambda i, j: (0, j)),
                pl.BlockSpec((tm, tn), lambda i, j: (i, gate_blk + j))]
    args = [a, w, z_sig]
    if prev is not None:
        in_specs.append(pl.BlockSpec((tm, tn), lambda i, j: (i, j)))
        args.append(prev)
    return pl.pallas_call(
        functools.partial(_gated_proj_kernel, has_prev=prev is not None),
        grid=(s // tm, n // tn),
        in_specs=in_specs,
        out_specs=pl.BlockSpec((tm, tn), lambda i, j: (i, j)),
        out_shape=jax.ShapeDtypeStruct((s, n), out_dtype),
        compiler_params=_params("parallel", "arbitrary"),
        name="gated_proj",
    )(*args)


def _out_proj_kernel(m_ref, w_ref, x_ref, g_ref, o_ref, *, tn):
    j = pl.program_id(1)
    col = pl.multiple_of(j * tn, tn)
    o_ref[:, pl.ds(col, tn)] = x_ref[...] + jnp.dot(m_ref[...], w_ref[...], preferred_element_type=F32)

    @pl.when(j == pl.num_programs(1) - 1)
    def _():
        y = o_ref[...]
        y = y * lax.rsqrt(jnp.mean(y * y, axis=-1, keepdims=True) + NORM_EPS)
        o_ref[...] = y * g_ref[...]


def _out_proj(merged, w, x, g, tm, tn):
    s, d = x.shape
    return pl.pallas_call(
        functools.partial(_out_proj_kernel, tn=tn),
        grid=(s // tm, d // tn),
        in_specs=[pl.BlockSpec((tm, d), lambda i, j: (i, 0)),
                  pl.BlockSpec((d, tn), lambda i, j: (0, j)),
                  pl.BlockSpec((tm, tn), lambda i, j: (i, j)),
                  pl.BlockSpec((1, d), lambda i, j: (0, 0))],
        out_specs=pl.BlockSpec((tm, d), lambda i, j: (i, 0)),
        out_shape=jax.ShapeDtypeStruct((s, d), F32),
        compiler_params=_params("parallel", "arbitrary"),
        name="out_proj_norm",
    )(merged, w, x, g.reshape(1, d))


def kernel(x, mem, norm_in, norm_mem, w_in, w_pool_group, pool_scale, w_mem_k, w_mem_v,
           w_proj_pool, w_proj_ret, w_proj_mem, w_out, norm_f):
    b, s, d = x.shape
    depth = w_in.shape[0]
    assert b == 1 and d == D_MODEL and depth == 1

    pos = jnp.arange(s, dtype=F32)
    inv = ROPE_BASE ** (-jnp.arange(ROPE_HALF, dtype=F32) / ROPE_HALF)
    ang = pos[:, None] * inv[None, :]
    cos, sin = jnp.cos(ang), jnp.sin(ang)
    lg = jnp.log1p(-(2.0 ** (-5.0 - jnp.arange(RET_HEADS, dtype=F32))))
    lg_tab = jnp.broadcast_to(lg[:, None, None], (RET_HEADS, 8, 128))

    xs = x[0]
    mems = mem[0]
    h = _rmsnorm(xs, norm_in[0], tm=256)
    tm_in, tn_in = 2048, 512
    gd = POOL_GROUP_DIM
    z_lin, wb_ret, wb_out, wb_group = _in_proj(
        h, w_in[0], LIN_SEGS, "linear", None, None, tm_in, tn_in,
        casts=(w_proj_ret[0], w_out[0], w_pool_group[0].reshape(POOL_GROUPS * gd, gd)))
    z_silu, wb_pool, wb_mem = _in_proj(
        h, w_in[0], SILU_SEGS, "silu", None, None, tm_in, tn_in, casts=(w_proj_pool[0], w_proj_mem[0]))
    (z_sig,) = _in_proj(h, w_in[0], SIG_SEGS, "sigmoid", None, None, tm_in, tn_in)
    z_qk, wb_mem_k, wb_mem_v = _in_proj(
        h, w_in[0], ROPE_SEGS, "rope", cos, sin, tm_in, tn_in, casts=(w_mem_k[0], w_mem_v[0]))

    pool_out = _pool_branch(z_lin, z_silu, wb_group.reshape(POOL_GROUPS, gd, gd), pool_scale[0], tm=1024)
    ret_out = _retention_branch(z_qk, z_lin, z_silu, lg_tab, chunk=256, n_chunks=4)

    memn = _rmsnorm(mems, norm_mem[0], tm=256)
    km = _matmul(memn, wb_mem_k, tm=256, tn=1024)
    vm = _matmul(memn, wb_mem_v, tm=256, tn=1024)
    mem_out = _mem_branch(z_lin, z_silu, km, vm, tm=1024)

    merged = _gated_proj(pool_out, wb_pool, z_sig, SIG_A_POOL, None, 1024, 1024, F32)
    merged = _gated_proj(ret_out, wb_ret, z_sig, SIG_A_RET, merged, 1024, 512, F32)
    merged = _gated_proj(mem_out, wb_mem, z_sig, SIG_A_MEM, merged, 1024, 1024, BF16)

    out = _out_proj(merged, wb_out, xs, norm_f, 512, 1024)
    return out[None]
```

```python
import functools
from typing import NamedTuple, Optional

import jax
import jax.numpy as jnp
from jax import lax
from jax.experimental import pallas as pl
from jax.experimental.pallas import tpu as pltpu

D_MODEL = 4096
POOL_WINDOWS = (2, 4, 8, 16)
POOL_GROUPS = 4
POOL_GROUP_DIM = D_MODEL // POOL_GROUPS
POOL_HALO = 32
RET_QK_DIM = 256
RET_HEADS = D_MODEL // RET_QK_DIM
RET_V_DIM = 2 * RET_QK_DIM
RET_V_WIDTH = RET_HEADS * RET_V_DIM
MEM_HEADS = 4
MEM_HEAD_DIM = D_MODEL // MEM_HEADS
ROPE_BASE = 10000.0
ROPE_HALF = RET_QK_DIM // 2
NORM_EPS = 1e-6

SEG_U_POOL, SEG_G_POOL, SEG_Q, SEG_K, SEG_V, SEG_G_RET, SEG_Q_MEM, SEG_G_MEM, SEG_A_POOL, SEG_A_RET, SEG_A_MEM = (
    0, 1, 2, 3, 4, 6, 8, 9, 10, 11, 12)
LIN_SEGS = (SEG_U_POOL, SEG_V, SEG_V + 1, SEG_Q_MEM)
LIN_U_POOL, LIN_V, LIN_Q_MEM = 0, 1, 3
SILU_SEGS = (SEG_G_POOL, SEG_G_RET, SEG_G_RET + 1, SEG_G_MEM)
SILU_G_POOL, SILU_G_RET, SILU_G_MEM = 0, 1, 3
SIG_SEGS = (SEG_A_POOL, SEG_A_RET, SEG_A_MEM)
SIG_A_POOL, SIG_A_RET, SIG_A_MEM = 0, 1, 2
ROPE_SEGS = (SEG_Q, SEG_K)
ROPE_Q, ROPE_K = 0, 1
ACT_POOL, ACT_RET, ACT_MEM = 0, D_MODEL, D_MODEL + RET_V_WIDTH
ACT_WIDTH = 2 * D_MODEL + RET_V_WIDTH

F32 = jnp.float32
BF16 = jnp.bfloat16
VMEM_LIMIT = 60 * 1024 * 1024


def _params(*sem):
    return pltpu.CompilerParams(dimension_semantics=sem, vmem_limit_bytes=VMEM_LIMIT)


def _sigmoid(x):
    return 0.5 * jnp.tanh(0.5 * x) + 0.5


def _rmsnorm_kernel(x_ref, g_ref, o_ref):
    x = x_ref[...]
    y = x * lax.rsqrt(jnp.mean(x * x, axis=-1, keepdims=True) + NORM_EPS)
    o_ref[...] = (y * g_ref[...]).astype(o_ref.dtype)


def _rmsnorm(x, g, tm):
    rows, d = x.shape
    return pl.pallas_call(
        _rmsnorm_kernel,
        grid=(rows // tm,),
        in_specs=[pl.BlockSpec((tm, d), lambda i: (i, 0)),
                  pl.BlockSpec((1, d), lambda i: (0, 0))],
        out_specs=pl.BlockSpec((tm, d), lambda i: (i, 0)),
        out_shape=jax.ShapeDtypeStruct((rows, d), BF16),
        compiler_params=_params("parallel"),
        name="rmsnorm",
    )(x, g.reshape(1, d))


class _Cast(NamedTuple):
    src: jax.Array
    dst_rows: int
    dst_offset: int = 0
    dst: Optional[jax.Array] = None


def _in_proj_kernel(*refs, epilogue, tn, sub_rows, n_casts, n_alias):
    h_ref, w_ref = refs[:2]
    n_in = 2 + (2 if epilogue == "rope" else 0)
    cast_in = refs[n_in:n_in + n_casts]
    o_ref = refs[n_in + n_casts + n_alias]
    cast_out = refs[n_in + n_casts + n_alias + 1:]
    w = w_ref[...].astype(BF16)
    for r in range(h_ref.shape[0] // sub_rows):
        rows = slice(r * sub_rows, (r + 1) * sub_rows)
        acc = jnp.dot(h_ref[rows, :], w, preferred_element_type=F32)
        if epilogue == "linear":
            o_ref[rows, :] = acc.astype(o_ref.dtype)
        elif epilogue == "silu":
            o_ref[rows, :] = (acc * _sigmoid(acc)).astype(o_ref.dtype)
        elif epilogue == "sigmoid":
            o_ref[rows, :] = _sigmoid(acc).astype(o_ref.dtype)
        else:
            assert epilogue == "rope"
            cos = refs[2][rows, :]
            sin = refs[3][rows, :]
            for hh in range(tn // RET_QK_DIM):
                lo = hh * RET_QK_DIM
                x1 = acc[:, lo:lo + ROPE_HALF]
                x2 = acc[:, lo + ROPE_HALF:lo + RET_QK_DIM]
                o_ref[rows, lo:lo + ROPE_HALF] = (x1 * cos - x2 * sin).astype(o_ref.dtype)
                o_ref[rows, lo + ROPE_HALF:lo + RET_QK_DIM] = (x2 * cos + x1 * sin).astype(o_ref.dtype)
    for src_ref, dst_ref in zip(cast_in, cast_out):
        dst_ref[...] = src_ref[...].astype(dst_ref.dtype)


def _in_proj(h, w, segs, epilogue, cos, sin, tm, tn, casts=()):
    s, d = h.shape
    nb = D_MODEL // tn
    nj = len(segs) * nb
    steps = (s // tm) * nj

    def w_block(j):
        js = j // nb
        seg = segs[-1]
        for idx in range(len(segs) - 2, -1, -1):
            seg = jnp.where(js == idx, segs[idx], seg)
        return seg * nb + j % nb

    in_specs = [pl.BlockSpec((tm, d), lambda i, j: (i, 0), pipeline_mode=pl.Buffered(1)),
                pl.BlockSpec((d, tn), lambda i, j: (0, w_block(j)))]
    args = [h, w]
    if epilogue == "rope":
        in_specs += [pl.BlockSpec((tm, ROPE_HALF), lambda i, j: (i, 0))] * 2
        args += [cos, sin]
    out_specs = [pl.BlockSpec((tm, tn), lambda i, j: (i, j))]
    out_shape = [jax.ShapeDtypeStruct((s, len(segs) * D_MODEL), BF16)]
    for c in casts:
        slab_rows, cols = c.src.shape[0] // steps, c.src.shape[1]
        first_slab = c.dst_offset // slab_rows
        in_specs.append(pl.BlockSpec((slab_rows, cols), lambda i, j: (i * nj + j, 0)))
        out_specs.append(pl.BlockSpec((slab_rows, cols), lambda i, j, first_slab=first_slab: (first_slab + i * nj + j, 0)))
        out_shape.append(jax.ShapeDtypeStruct((c.dst_rows, cols), BF16))
        args.append(c.src)
    aliases = {}
    for out_idx, c in enumerate(casts, start=1):
        if c.dst is not None:
            aliases[len(args)] = out_idx
            in_specs.append(pl.BlockSpec(memory_space=pl.ANY))
            args.append(c.dst)
    return pl.pallas_call(
        functools.partial(_in_proj_kernel, epilogue=epilogue, tn=tn, sub_rows=256, n_casts=len(casts),
                          n_alias=len(aliases)),
        grid=(s // tm, nj),
        in_specs=in_specs,
        out_specs=out_specs,
        out_shape=out_shape,
        input_output_aliases=aliases,
        compiler_params=_params("parallel", "arbitrary"),
        name="in_proj_" + epilogue,
    )(*args)


def _matmul_kernel(a_ref, b_ref, o_ref):
    o_ref[...] = jnp.dot(a_ref[...], b_ref[...], preferred_element_type=F32).astype(o_ref.dtype)


def _matmul(a, b, tm, tn):
    m, k = a.shape
    n = b.shape[1]
    return pl.pallas_call(
        _matmul_kernel,
        grid=(m // tm, n // tn),
        in_specs=[pl.BlockSpec((tm, k), lambda i, j: (i, 0)),
                  pl.BlockSpec((k, tn), lambda i, j: (0, j))],
        out_specs=pl.BlockSpec((tm, tn), lambda i, j: (i, j)),
        out_shape=jax.ShapeDtypeStruct((m, n), BF16),
        compiler_params=_params("parallel", "arbitrary"),
        name="mem_kv_proj",
    )(a, b)


def _pool_kernel(u_ref, halo_ref, gate_ref, w_ref, scale_ref, o_ref, buf_a, buf_b, *, tm):
    g = pl.program_id(0)
    i = pl.program_id(1)
    u = u_ref[...].astype(F32)
    halo = halo_ref[...].astype(F32)
    end = POOL_HALO + tm
    buf_a[0:POOL_HALO, :] = jnp.where(i == 0, 0.0, halo)
    buf_a[POOL_HALO:end, :] = u
    t = i * tm + lax.broadcasted_iota(jnp.int32, (tm, 1), 0)

    for gi, win in enumerate(POOL_WINDOWS):
        @pl.when(g == gi)
        def _(win=win):
            src, dst = buf_a, buf_b
            span, lo = 1, 8
            while span < win:
                dst[lo:end, :] = src[lo:end, :] + src[lo - span:end - span, :]
                src, dst = dst, src
                span, lo = 2 * span, lo + 8
            wsum = src[POOL_HALO:end, :]
            cnt = jnp.minimum(t + 1, win).astype(F32)
            mixed = wsum / cnt - u
            y = jnp.dot(mixed.astype(BF16), w_ref[0], preferred_element_type=F32)
            y = y * scale_ref[...] * gate_ref[...].astype(F32)
            o_ref[...] = y.astype(o_ref.dtype)


def _pool_branch(z_lin, z_silu, w_group, scale, tm):
    s = z_lin.shape[0]
    gd = POOL_GROUP_DIM
    halo_blocks_per_tile = tm // POOL_HALO
    return pl.pallas_call(
        functools.partial(_pool_kernel, tm=tm),
        grid=(POOL_GROUPS, s // tm),
        in_specs=[
            pl.BlockSpec((tm, gd), lambda g, i: (i, LIN_U_POOL * POOL_GROUPS + g)),
            pl.BlockSpec((POOL_HALO, gd),
                         lambda g, i: (jnp.maximum(i * halo_blocks_per_tile - 1, 0), LIN_U_POOL * POOL_GROUPS + g)),
            pl.BlockSpec((tm, gd), lambda g, i: (i, SILU_G_POOL * POOL_GROUPS + g)),
            pl.BlockSpec((1, gd, gd), lambda g, i: (g, 0, 0)),
            pl.BlockSpec((1, gd), lambda g, i: (0, g)),
        ],
        out_specs=pl.BlockSpec((tm, gd), lambda g, i: (i, ACT_POOL // gd + g)),
        out_shape=jax.ShapeDtypeStruct((s, ACT_WIDTH), BF16),
        scratch_shapes=[pltpu.VMEM((tm + POOL_HALO, gd), F32)] * 2,
        compiler_params=_params("parallel", "arbitrary"),
        name="pool_branch",
    )(z_lin, z_lin, z_silu, w_group, scale.reshape(1, D_MODEL))


def _retention_kernel(lg_ref, q_ref, k_ref, v_ref, gate_ref, acts_ref, o_ref, state_ref, *, chunk, n_chunks):
    del acts_ref
    @pl.when(pl.program_id(1) == 0)
    def _():
        state_ref[...] = jnp.zeros_like(state_ref)

    k_scale = RET_QK_DIM ** -0.5
    lg = lg_ref[0][0:1, 0:1]
    row = lax.broadcasted_iota(jnp.int32, (chunk, 1), 0).astype(F32)
    col = lax.broadcasted_iota(jnp.int32, (1, chunk), 1).astype(F32)
    diff = row - col
    intra = jnp.where(diff >= 0, jnp.exp(jnp.maximum(diff, 0.0) * lg) * k_scale, 0.0)
    q_decay = jnp.exp((row + 1.0) * lg)
    k_decay = jnp.exp((chunk - 1.0 - row) * lg) * k_scale
    chunk_decay = jnp.exp(chunk * lg)

    for c in range(n_chunks):
        rows = slice(c * chunk, (c + 1) * chunk)
        q = q_ref[rows, :]
        k = k_ref[rows, :]
        v = v_ref[rows, :]
        state = state_ref[...]
        scores = lax.dot_general(q, k, (((1,), (1,)), ((), ())), preferred_element_type=F32) * intra
        inner = jnp.dot(scores.astype(BF16), v, preferred_element_type=F32)
        cross = jnp.dot(q, state.astype(BF16), preferred_element_type=F32) * q_decay
        kd = (k.astype(F32) * k_decay).astype(BF16)
        state_ref[...] = state * chunk_decay + lax.dot_general(
            kd, v, (((0,), (0,)), ((), ())), preferred_element_type=F32)

        o = inner + cross
        o = o * lax.rsqrt(jnp.mean(o * o, axis=-1, keepdims=True) + NORM_EPS)
        o_ref[rows, :] = (o * gate_ref[rows, :].astype(F32)).astype(o_ref.dtype)


def _retention_branch(z_qk, z_lin, z_silu, lg_tab, acts, chunk, n_chunks):
    s = z_qk.shape[0]
    rows = chunk * n_chunks
    q_blk = ROPE_Q * D_MODEL // RET_QK_DIM
    k_blk = ROPE_K * D_MODEL // RET_QK_DIM
    v_blk = LIN_V * D_MODEL // RET_V_DIM
    g_blk = SILU_G_RET * D_MODEL // RET_V_DIM
    return pl.pallas_call(
        functools.partial(_retention_kernel, chunk=chunk, n_chunks=n_chunks),
        grid=(RET_HEADS, s // rows),
        in_specs=[
            pl.BlockSpec((1, 8, 128), lambda h, n: (h, 0, 0)),
            pl.BlockSpec((rows, RET_QK_DIM), lambda h, n: (n, q_blk + h)),
            pl.BlockSpec((rows, RET_QK_DIM), lambda h, n: (n, k_blk + h)),
            pl.BlockSpec((rows, RET_V_DIM), lambda h, n: (n, v_blk + h)),
            pl.BlockSpec((rows, RET_V_DIM), lambda h, n: (n, g_blk + h)),
            pl.BlockSpec(memory_space=pl.ANY),
        ],
        out_specs=pl.BlockSpec((rows, RET_V_DIM), lambda h, n: (n, ACT_RET // RET_V_DIM + h)),
        out_shape=jax.ShapeDtypeStruct(acts.shape, acts.dtype),
        input_output_aliases={5: 0},
        scratch_shapes=[pltpu.VMEM((RET_QK_DIM, RET_V_DIM), F32)],
        compiler_params=_params("parallel", "arbitrary"),
        name="retention_branch",
    )(lg_tab, z_qk, z_qk, z_lin, z_silu, acts)


def _mem_attn_kernel(q_ref, gate_ref, k_ref, v_ref, acts_ref, o_ref):
    del acts_ref
    scores = lax.dot_general(q_ref[...], k_ref[...], (((1,), (1,)), ((), ())),
                             preferred_element_type=F32) * (MEM_HEAD_DIM ** -0.5)
    e = jnp.exp(scores - jnp.max(scores, axis=-1, keepdims=True))
    p = e * (1.0 / jnp.sum(e, axis=-1, keepdims=True))
    mo = jnp.dot(p.astype(BF16), v_ref[...], preferred_element_type=F32)
    o_ref[...] = (mo * gate_ref[...].astype(F32)).astype(o_ref.dtype)


def _mem_branch(z_lin, z_silu, km, vm, acts, tm):
    s = z_lin.shape[0]
    m = km.shape[0]
    hd = MEM_HEAD_DIM
    return pl.pallas_call(
        _mem_attn_kernel,
        grid=(MEM_HEADS, s // tm),
        in_specs=[
            pl.BlockSpec((tm, hd), lambda h, i: (i, LIN_Q_MEM * MEM_HEADS + h)),
            pl.BlockSpec((tm, hd), lambda h, i: (i, SILU_G_MEM * MEM_HEADS + h)),
            pl.BlockSpec((m, hd), lambda h, i: (0, h)),
            pl.BlockSpec((m, hd), lambda h, i: (0, h)),
            pl.BlockSpec(memory_space=pl.ANY),
        ],
        out_specs=pl.BlockSpec((tm, hd), lambda h, i: (i, ACT_MEM // hd + h)),
        out_shape=jax.ShapeDtypeStruct(acts.shape, acts.dtype),
        input_output_aliases={4: 0},
        compiler_params=_params("parallel", "arbitrary"),
        name="mem_branch",
    )(z_lin, z_silu, km, vm, acts)


def _merge_kernel(a_ref, w_ref, gate_ref, o_ref, acc_ref, *, sub_rows):
    kb = pl.program_id(2)
    last = pl.num_programs(2) - 1

    def accumulate(first, final):
        w = w_ref[...]
        for r in range(a_ref.shape[0] // sub_rows):
            rows = slice(r * sub_rows, (r + 1) * sub_rows)
            acc = jnp.dot(a_ref[rows, :], w, preferred_element_type=F32) * gate_ref[rows, :].astype(F32)
            if not first:
                acc = acc + acc_ref[rows, :]
            if final:
                o_ref[rows, :] = acc.astype(o_ref.dtype)
            else:
                acc_ref[rows, :] = acc

    @pl.when(kb == 0)
    def _():
        accumulate(True, False)

    @pl.when((kb > 0) & (kb < last))
    def _():
        accumulate(False, False)

    @pl.when(kb == last)
    def _():
        accumulate(False, True)


def _merge_branches(acts, w_cat, z_sig, tm, tn):
    s = acts.shape[0]
    n_chunks = ACT_WIDTH // D_MODEL
    nb = D_MODEL // tn

    def gate_block(kb, j):
        return ((kb + 1) // 2) * nb + j

    return pl.pallas_call(
        functools.partial(_merge_kernel, sub_rows=256),
        grid=(s // tm, nb, n_chunks),
        in_specs=[pl.BlockSpec((tm, D_MODEL), lambda i, j, kb: (i, kb)),
                  pl.BlockSpec((D_MODEL, tn), lambda i, j, kb: (kb, j)),
                  pl.BlockSpec((tm, tn), lambda i, j, kb: (i, gate_block(kb, j)))],
        out_specs=pl.BlockSpec((tm, tn), lambda i, j, kb: (i, j)),
        out_shape=jax.ShapeDtypeStruct((s, D_MODEL), BF16),
        scratch_shapes=[pltpu.VMEM((tm, tn), F32)],
        compiler_params=_params("parallel", "parallel", "arbitrary"),
        name="merge_branches",
    )(acts, w_cat, z_sig)


def _out_proj_kernel(m_ref, w_ref, x_ref, g_ref, o_ref, *, tn):
    j = pl.program_id(1)
    col = pl.multiple_of(j * tn, tn)
    o_ref[:, pl.ds(col, tn)] = x_ref[...] + jnp.dot(m_ref[...], w_ref[...], preferred_element_type=F32)

    @pl.when(j == pl.num_programs(1) - 1)
    def _():
        y = o_ref[...]
        y = y * lax.rsqrt(jnp.mean(y * y, axis=-1, keepdims=True) + NORM_EPS)
        o_ref[...] = y * g_ref[...]


def _out_proj(merged, w, x, g, tm, tn):
    s, d = x.shape
    return pl.pallas_call(
        functools.partial(_out_proj_kernel, tn=tn),
        grid=(s // tm, d // tn),
        in_specs=[pl.BlockSpec((tm, d), lambda i, j: (i, 0)),
                  pl.BlockSpec((d, tn), lambda i, j: (0, j)),
                  pl.BlockSpec((tm, tn), lambda i, j: (i, j)),
                  pl.BlockSpec((1, d), lambda i, j: (0, 0))],
        out_specs=pl.BlockSpec((tm, d), lambda i, j: (i, 0)),
        out_shape=jax.ShapeDtypeStruct((s, d), F32),
        compiler_params=_params("parallel", "arbitrary"),
        name="out_proj_norm",
    )(merged, w, x, g.reshape(1, d))


def kernel(x, mem, norm_in, norm_mem, w_in, w_pool_group, pool_scale, w_mem_k, w_mem_v,
           w_proj_pool, w_proj_ret, w_proj_mem, w_out, norm_f):
    b, s, d = x.shape
    depth = w_in.shape[0]
    assert b == 1 and d == D_MODEL and depth == 1

    pos = jnp.arange(s, dtype=F32)
    inv = ROPE_BASE ** (-jnp.arange(ROPE_HALF, dtype=F32) / ROPE_HALF)
    ang = pos[:, None] * inv[None, :]
    cos, sin = jnp.cos(ang), jnp.sin(ang)
    lg = jnp.log1p(-(2.0 ** (-5.0 - jnp.arange(RET_HEADS, dtype=F32))))
    lg_tab = jnp.broadcast_to(lg[:, None, None], (RET_HEADS, 8, 128))

    xs = x[0]
    mems = mem[0]
    h = _rmsnorm(xs, norm_in[0], tm=256)
    tm_in, tn_in = 2048, 512
    gd = POOL_GROUP_DIM
    z_lin, w_cat, wb_out, wb_group = _in_proj(
        h, w_in[0], LIN_SEGS, "linear", None, None, tm_in, tn_in,
        casts=(_Cast(w_proj_ret[0], ACT_WIDTH, ACT_RET),
               _Cast(w_out[0], D_MODEL),
               _Cast(w_pool_group[0].reshape(POOL_GROUPS * gd, gd), POOL_GROUPS * gd)))
    z_silu, w_cat, wb_mem_k, wb_mem_v = _in_proj(
        h, w_in[0], SILU_SEGS, "silu", None, None, tm_in, tn_in,
        casts=(_Cast(w_proj_pool[0], ACT_WIDTH, ACT_POOL, w_cat),
               _Cast(w_mem_k[0], D_MODEL),
               _Cast(w_mem_v[0], D_MODEL)))
    (z_sig,) = _in_proj(h, w_in[0], SIG_SEGS, "sigmoid", None, None, tm_in, tn_in)
    z_qk, w_cat = _in_proj(
        h, w_in[0], ROPE_SEGS, "rope", cos, sin, tm_in, tn_in,
        casts=(_Cast(w_proj_mem[0], ACT_WIDTH, ACT_MEM, w_cat),))

    acts = _pool_branch(z_lin, z_silu, wb_group.reshape(POOL_GROUPS, gd, gd), pool_scale[0], tm=1024)
    acts = _retention_branch(z_qk, z_lin, z_silu, lg_tab, acts, chunk=256, n_chunks=8)

    memn = _rmsnorm(mems, norm_mem[0], tm=256)
    km = _matmul(memn, wb_mem_k, tm=256, tn=1024)
    vm = _matmul(memn, wb_mem_v, tm=256, tn=1024)
    acts = _mem_branch(z_lin, z_silu, km, vm, acts, tm=1024)

    merged = _merge_branches(acts, w_cat, z_sig, tm=1024, tn=1024)
    out = _out_proj(merged, wb_out, xs, norm_f, 512, 1024)
    return out[None]
```

```python
import functools
from typing import NamedTuple, Optional

import jax
import jax.numpy as jnp
from jax import lax
from jax.experimental import pallas as pl
from jax.experimental.pallas import tpu as pltpu

D_MODEL = 4096
POOL_WINDOWS = (2, 4, 8, 16)
POOL_GROUPS = 4
POOL_GROUP_DIM = D_MODEL // POOL_GROUPS
POOL_HALO = 32
RET_QK_DIM = 256
RET_HEADS = D_MODEL // RET_QK_DIM
RET_V_DIM = 2 * RET_QK_DIM
RET_V_WIDTH = RET_HEADS * RET_V_DIM
MEM_HEADS = 4
MEM_HEAD_DIM = D_MODEL // MEM_HEADS
ROPE_BASE = 10000.0
ROPE_HALF = RET_QK_DIM // 2
NORM_EPS = 1e-6

SEG_U_POOL, SEG_G_POOL, SEG_Q, SEG_K, SEG_V, SEG_G_RET, SEG_Q_MEM, SEG_G_MEM, SEG_A_POOL, SEG_A_RET, SEG_A_MEM = (
    0, 1, 2, 3, 4, 6, 8, 9, 10, 11, 12)
LIN_SEGS = (SEG_U_POOL, SEG_V, SEG_V + 1, SEG_Q_MEM)
LIN_U_POOL, LIN_V, LIN_Q_MEM = 0, 1, 3
SILU_SEGS = (SEG_G_POOL, SEG_G_RET, SEG_G_RET + 1, SEG_G_MEM)
SILU_G_POOL, SILU_G_RET, SILU_G_MEM = 0, 1, 3
SIG_SEGS = (SEG_A_POOL, SEG_A_RET, SEG_A_MEM)
SIG_A_POOL, SIG_A_RET, SIG_A_MEM = 0, 1, 2
ROPE_SEGS = (SEG_Q, SEG_K)
ROPE_Q, ROPE_K = 0, 1
ACT_POOL, ACT_RET, ACT_MEM = 0, D_MODEL, D_MODEL + RET_V_WIDTH
ACT_WIDTH = 2 * D_MODEL + RET_V_WIDTH

F32 = jnp.float32
BF16 = jnp.bfloat16
VMEM_LIMIT = 60 * 1024 * 1024


def _params(*sem):
    return pltpu.CompilerParams(dimension_semantics=sem, vmem_limit_bytes=VMEM_LIMIT)


def _sigmoid(x):
    return 0.5 * jnp.tanh(0.5 * x) + 0.5


def _rmsnorm_kernel(x_ref, g_ref, o_ref):
    x = x_ref[...]
    y = x * lax.rsqrt(jnp.mean(x * x, axis=-1, keepdims=True) + NORM_EPS)
    o_ref[...] = (y * g_ref[...]).astype(o_ref.dtype)


def _rmsnorm(x, g, tm):
    rows, d = x.shape
    return pl.pallas_call(
        _rmsnorm_kernel,
        grid=(rows // tm,),
        in_specs=[pl.BlockSpec((tm, d), lambda i: (i, 0)),
                  pl.BlockSpec((1, d), lambda i: (0, 0))],
        out_specs=pl.BlockSpec((tm, d), lambda i: (i, 0)),
        out_shape=jax.ShapeDtypeStruct((rows, d), BF16),
        compiler_params=_params("parallel"),
        name="rmsnorm",
    )(x, g.reshape(1, d))


class _Cast(NamedTuple):
    src: jax.Array
    dst_rows: int
    dst_offset: int = 0
    dst: Optional[jax.Array] = None


def _in_proj_kernel(*refs, epilogue, tn, sub_rows, n_casts, n_alias, emit_w):
    h_ref, w_ref = refs[:2]
    n_in = 2 + (2 if epilogue == "rope" else 0)
    cast_in = refs[n_in:n_in + n_casts]
    outs = refs[n_in + n_casts + n_alias:]
    o_ref = outs[0]
    cast_out = outs[1 + emit_w:]
    if emit_w:
        outs[1][...] = w_ref[...].astype(BF16)
        w_ref = outs[1]
    for r in range(h_ref.shape[0] // sub_rows):
        rows = slice(r * sub_rows, (r + 1) * sub_rows)
        acc = jnp.dot(h_ref[rows, :], w_ref[...], preferred_element_type=F32)
        if epilogue == "linear":
            o_ref[rows, :] = acc.astype(o_ref.dtype)
        elif epilogue == "silu":
            o_ref[rows, :] = (acc * _sigmoid(acc)).astype(o_ref.dtype)
        elif epilogue == "sigmoid":
            o_ref[rows, :] = _sigmoid(acc).astype(o_ref.dtype)
        else:
            assert epilogue == "rope"
            cos = refs[2][rows, :]
            sin = refs[3][rows, :]
            for hh in range(tn // RET_QK_DIM):
                lo = hh * RET_QK_DIM
                x1 = acc[:, lo:lo + ROPE_HALF]
                x2 = acc[:, lo + ROPE_HALF:lo + RET_QK_DIM]
                o_ref[rows, lo:lo + ROPE_HALF] = (x1 * cos - x2 * sin).astype(o_ref.dtype)
                o_ref[rows, lo + ROPE_HALF:lo + RET_QK_DIM] = (x2 * cos + x1 * sin).astype(o_ref.dtype)
    for src_ref, dst_ref in zip(cast_in, cast_out):
        dst_ref[...] = src_ref[...].astype(dst_ref.dtype)


def _in_proj(h, w, segs, epilogue, cos, sin, *, tm, tn, row_blocks, first_row_block=0, z=None, casts=()):
    s, d = h.shape
    nb = D_MODEL // tn
    nj = len(segs) * nb
    steps = row_blocks * nj
    emit_w = w.dtype == F32

    def w_block(j):
        if not emit_w:
            return j
        js = j // nb
        seg = segs[-1]
        for idx in range(len(segs) - 2, -1, -1):
            seg = jnp.where(js == idx, segs[idx], seg)
        return seg * nb + j % nb

    def row_block(i):
        return first_row_block + i

    in_specs = [pl.BlockSpec((tm, d), lambda i, j: (row_block(i), 0), pipeline_mode=pl.Buffered(1)),
                pl.BlockSpec((d, tn), lambda i, j: (0, w_block(j)))]
    args = [h, w]
    if epilogue == "rope":
        in_specs += [pl.BlockSpec((tm, ROPE_HALF), lambda i, j: (row_block(i), 0))] * 2
        args += [cos, sin]
    out_specs = [pl.BlockSpec((tm, tn), lambda i, j: (row_block(i), j))]
    out_shape = [jax.ShapeDtypeStruct((s, len(segs) * D_MODEL), BF16)]
    if emit_w:
        out_specs.append(pl.BlockSpec((d, tn), lambda i, j: (0, j)))
        out_shape.append(jax.ShapeDtypeStruct((d, len(segs) * D_MODEL), BF16))
    for c in casts:
        n_slabs = 1 << (steps.bit_length() - 1)
        slab_rows, cols = c.src.shape[0] // n_slabs, c.src.shape[1]
        first_slab = c.dst_offset // slab_rows

        def slab(i, j, n_slabs=n_slabs):
            return jnp.minimum(i * nj + j, n_slabs - 1)

        in_specs.append(pl.BlockSpec((slab_rows, cols), lambda i, j, slab=slab: (slab(i, j), 0)))
        out_specs.append(pl.BlockSpec((slab_rows, cols),
                                      lambda i, j, slab=slab, first_slab=first_slab: (first_slab + slab(i, j), 0)))
        out_shape.append(jax.ShapeDtypeStruct((c.dst_rows, cols), BF16))
        args.append(c.src)
    n_cast_args = len(args)
    aliases = {}
    if z is not None:
        aliases[len(args)] = 0
        in_specs.append(pl.BlockSpec(memory_space=pl.ANY))
        args.append(z)
    for out_idx, c in enumerate(casts, start=1 + emit_w):
        if c.dst is not None:
            aliases[len(args)] = out_idx
            in_specs.append(pl.BlockSpec(memory_space=pl.ANY))
            args.append(c.dst)
    return pl.pallas_call(
        functools.partial(_in_proj_kernel, epilogue=epilogue, tn=tn, sub_rows=256, n_casts=len(casts),
                          n_alias=len(args) - n_cast_args, emit_w=emit_w),
        grid=(row_blocks, nj),
        in_specs=in_specs,
        out_specs=out_specs,
        out_shape=out_shape,
        input_output_aliases=aliases,
        compiler_params=_params("parallel", "arbitrary"),
        name="in_proj_" + epilogue + ("_first" if emit_w else "_rest"),
    )(*args)


def _matmul_kernel(a_ref, b_ref, o_ref):
    o_ref[...] = jnp.dot(a_ref[...], b_ref[...], preferred_element_type=F32).astype(o_ref.dtype)


def _matmul(a, b, tm, tn):
    m, k = a.shape
    n = b.shape[1]
    return pl.pallas_call(
        _matmul_kernel,
        grid=(m // tm, n // tn),
        in_specs=[pl.BlockSpec((tm, k), lambda i, j: (i, 0)),
                  pl.BlockSpec((k, tn), lambda i, j: (0, j))],
        out_specs=pl.BlockSpec((tm, tn), lambda i, j: (i, j)),
        out_shape=jax.ShapeDtypeStruct((m, n), BF16),
        compiler_params=_params("parallel", "arbitrary"),
        name="mem_kv_proj",
    )(a, b)


def _pool_kernel(u_ref, halo_ref, gate_ref, w_ref, scale_ref, o_ref, buf_a, buf_b, *, tm):
    g = pl.program_id(0)
    i = pl.program_id(1)
    u = u_ref[...].astype(F32)
    halo = halo_ref[...].astype(F32)
    end = POOL_HALO + tm
    buf_a[0:POOL_HALO, :] = jnp.where(i == 0, 0.0, halo)
    buf_a[POOL_HALO:end, :] = u
    t = i * tm + lax.broadcasted_iota(jnp.int32, (tm, 1), 0)

    for gi, win in enumerate(POOL_WINDOWS):
        @pl.when(g == gi)
        def _(win=win):
            src, dst = buf_a, buf_b
            span, lo = 1, 8
            while span < win:
                dst[lo:end, :] = src[lo:end, :] + src[lo - span:end - span, :]
                src, dst = dst, src
                span, lo = 2 * span, lo + 8
            wsum = src[POOL_HALO:end, :]
            cnt = jnp.minimum(t + 1, win).astype(F32)
            mixed = wsum / cnt - u
            y = jnp.dot(mixed.astype(BF16), w_ref[0], preferred_element_type=F32)
            y = y * scale_ref[...] * gate_ref[...].astype(F32)
            o_ref[...] = y.astype(o_ref.dtype)


def _pool_branch(z_lin, z_silu, w_group, scale, tm):
    s = z_lin.shape[0]
    gd = POOL_GROUP_DIM
    halo_blocks_per_tile = tm // POOL_HALO
    return pl.pallas_call(
        functools.partial(_pool_kernel, tm=tm),
        grid=(POOL_GROUPS, s // tm),
        in_specs=[
            pl.BlockSpec((tm, gd), lambda g, i: (i, LIN_U_POOL * POOL_GROUPS + g)),
            pl.BlockSpec((POOL_HALO, gd),
                         lambda g, i: (jnp.maximum(i * halo_blocks_per_tile - 1, 0), LIN_U_POOL * POOL_GROUPS + g)),
            pl.BlockSpec((tm, gd), lambda g, i: (i, SILU_G_POOL * POOL_GROUPS + g)),
            pl.BlockSpec((1, gd, gd), lambda g, i: (g, 0, 0)),
            pl.BlockSpec((1, gd), lambda g, i: (0, g)),
        ],
        out_specs=pl.BlockSpec((tm, gd), lambda g, i: (i, ACT_POOL // gd + g)),
        out_shape=jax.ShapeDtypeStruct((s, ACT_WIDTH), BF16),
        scratch_shapes=[pltpu.VMEM((tm + POOL_HALO, gd), F32)] * 2,
        compiler_params=_params("parallel", "arbitrary"),
        name="pool_branch",
    )(z_lin, z_lin, z_silu, w_group, scale.reshape(1, D_MODEL))


def _retention_kernel(lg_ref, q_ref, k_ref, v_ref, gate_ref, acts_ref, o_ref, state_ref, *, chunk, n_chunks):
    del acts_ref
    @pl.when(pl.program_id(1) == 0)
    def _():
        state_ref[...] = jnp.zeros_like(state_ref)

    k_scale = RET_QK_DIM ** -0.5
    lg = lg_ref[0][0:1, 0:1]
    row = lax.broadcasted_iota(jnp.int32, (chunk, 1), 0).astype(F32)
    col = lax.broadcasted_iota(jnp.int32, (1, chunk), 1).astype(F32)
    diff = row - col
    intra = jnp.where(diff >= 0, jnp.exp(jnp.maximum(diff, 0.0) * lg) * k_scale, 0.0)
    q_decay = jnp.exp((row + 1.0) * lg)
    k_decay = jnp.exp((chunk - 1.0 - row) * lg) * k_scale
    chunk_decay = jnp.exp(chunk * lg)

    for c in range(n_chunks):
        rows = slice(c * chunk, (c + 1) * chunk)
        q = q_ref[rows, :]
        k = k_ref[rows, :]
        v = v_ref[rows, :]
        state = state_ref[...]
        scores = lax.dot_general(q, k, (((1,), (1,)), ((), ())), preferred_element_type=F32) * intra
        inner = jnp.dot(scores.astype(BF16), v, preferred_element_type=F32)
        cross = jnp.dot(q, state.astype(BF16), preferred_element_type=F32) * q_decay
        kd = (k.astype(F32) * k_decay).astype(BF16)
        state_ref[...] = state * chunk_decay + lax.dot_general(
            kd, v, (((0,), (0,)), ((), ())), preferred_element_type=F32)

        o = inner + cross
        o = o * lax.rsqrt(jnp.mean(o * o, axis=-1, keepdims=True) + NORM_EPS)
        o_ref[rows, :] = (o * gate_ref[rows, :].astype(F32)).astype(o_ref.dtype)


def _retention_branch(z_qk, z_lin, z_silu, lg_tab, acts, chunk, n_chunks):
    s = z_qk.shape[0]
    rows = chunk * n_chunks
    q_blk = ROPE_Q * D_MODEL // RET_QK_DIM
    k_blk = ROPE_K * D_MODEL // RET_QK_DIM
    v_blk = LIN_V * D_MODEL // RET_V_DIM
    g_blk = SILU_G_RET * D_MODEL // RET_V_DIM
    return pl.pallas_call(
        functools.partial(_retention_kernel, chunk=chunk, n_chunks=n_chunks),
        grid=(RET_HEADS, s // rows),
        in_specs=[
            pl.BlockSpec((1, 8, 128), lambda h, n: (h, 0, 0)),
            pl.BlockSpec((rows, RET_QK_DIM), lambda h, n: (n, q_blk + h)),
            pl.BlockSpec((rows, RET_QK_DIM), lambda h, n: (n, k_blk + h)),
            pl.BlockSpec((rows, RET_V_DIM), lambda h, n: (n, v_blk + h)),
            pl.BlockSpec((rows, RET_V_DIM), lambda h, n: (n, g_blk + h)),
            pl.BlockSpec(memory_space=pl.ANY),
        ],
        out_specs=pl.BlockSpec((rows, RET_V_DIM), lambda h, n: (n, ACT_RET // RET_V_DIM + h)),
        out_shape=jax.ShapeDtypeStruct(acts.shape, acts.dtype),
        input_output_aliases={5: 0},
        scratch_shapes=[pltpu.VMEM((RET_QK_DIM, RET_V_DIM), F32)],
        compiler_params=_params("parallel", "arbitrary"),
        name="retention_branch",
    )(lg_tab, z_qk, z_qk, z_lin, z_silu, acts)


def _mem_attn_kernel(q_ref, gate_ref, k_ref, v_ref, acts_ref, o_ref):
    del acts_ref
    scores = lax.dot_general(q_ref[...], k_ref[...], (((1,), (1,)), ((), ())),
                             preferred_element_type=F32) * (MEM_HEAD_DIM ** -0.5)
    e = jnp.exp(scores - jnp.max(scores, axis=-1, keepdims=True))
    p = e * (1.0 / jnp.sum(e, axis=-1, keepdims=True))
    mo = jnp.dot(p.astype(BF16), v_ref[...], preferred_element_type=F32)
    o_ref[...] = (mo * gate_ref[...].astype(F32)).astype(o_ref.dtype)


def _mem_branch(z_lin, z_silu, km, vm, acts, tm):
    s = z_lin.shape[0]
    m = km.shape[0]
    hd = MEM_HEAD_DIM
    return pl.pallas_call(
        _mem_attn_kernel,
        grid=(MEM_HEADS, s // tm),
        in_specs=[
            pl.BlockSpec((tm, hd), lambda h, i: (i, LIN_Q_MEM * MEM_HEADS + h)),
            pl.BlockSpec((tm, hd), lambda h, i: (i, SILU_G_MEM * MEM_HEADS + h)),
            pl.BlockSpec((m, hd), lambda h, i: (0, h)),
            pl.BlockSpec((m, hd), lambda h, i: (0, h)),
            pl.BlockSpec(memory_space=pl.ANY),
        ],
        out_specs=pl.BlockSpec((tm, hd), lambda h, i: (i, ACT_MEM // hd + h)),
        out_shape=jax.ShapeDtypeStruct(acts.shape, acts.dtype),
        input_output_aliases={4: 0},
        compiler_params=_params("parallel", "arbitrary"),
        name="mem_branch",
    )(z_lin, z_silu, km, vm, acts)


def _merge_kernel(a_ref, w_ref, gate_ref, o_ref, acc_ref, *, sub_rows):
    kb = pl.program_id(2)
    last = pl.num_programs(2) - 1

    def accumulate(first, final):
        w = w_ref[...]
        for r in range(a_ref.shape[0] // sub_rows):
            rows = slice(r * sub_rows, (r + 1) * sub_rows)
            acc = jnp.dot(a_ref[rows, :], w, preferred_element_type=F32) * gate_ref[rows, :].astype(F32)
            if not first:
                acc = acc + acc_ref[rows, :]
            if final:
                o_ref[rows, :] = acc.astype(o_ref.dtype)
            else:
                acc_ref[rows, :] = acc

    @pl.when(kb == 0)
    def _():
        accumulate(True, False)

    @pl.when((kb > 0) & (kb < last))
    def _():
        accumulate(False, False)

    @pl.when(kb == last)
    def _():
        accumulate(False, True)


def _merge_branches(acts, w_cat, z_sig, tm, tn):
    s = acts.shape[0]
    n_chunks = ACT_WIDTH // D_MODEL
    nb = D_MODEL // tn

    def gate_block(kb, j):
        return ((kb + 1) // 2) * nb + j

    return pl.pallas_call(
        functools.partial(_merge_kernel, sub_rows=256),
        grid=(s // tm, nb, n_chunks),
        in_specs=[pl.BlockSpec((tm, D_MODEL), lambda i, j, kb: (i, kb)),
                  pl.BlockSpec((D_MODEL, tn), lambda i, j, kb: (kb, j)),
                  pl.BlockSpec((tm, tn), lambda i, j, kb: (i, gate_block(kb, j)))],
        out_specs=pl.BlockSpec((tm, tn), lambda i, j, kb: (i, j)),
        out_shape=jax.ShapeDtypeStruct((s, D_MODEL), BF16),
        scratch_shapes=[pltpu.VMEM((tm, tn), F32)],
        compiler_params=_params("parallel", "parallel", "arbitrary"),
        name="merge_branches",
    )(acts, w_cat, z_sig)


def _out_proj_kernel(m_ref, w_ref, x_ref, g_ref, o_ref, *, tn):
    j = pl.program_id(1)
    col = pl.multiple_of(j * tn, tn)
    o_ref[:, pl.ds(col, tn)] = x_ref[...] + jnp.dot(m_ref[...], w_ref[...], preferred_element_type=F32)

    @pl.when(j == pl.num_programs(1) - 1)
    def _():
        y = o_ref[...]
        y = y * lax.rsqrt(jnp.mean(y * y, axis=-1, keepdims=True) + NORM_EPS)
        o_ref[...] = y * g_ref[...]


def _out_proj(merged, w, x, g, tm, tn):
    s, d = x.shape
    return pl.pallas_call(
        functools.partial(_out_proj_kernel, tn=tn),
        grid=(s // tm, d // tn),
        in_specs=[pl.BlockSpec((tm, d), lambda i, j: (i, 0)),
                  pl.BlockSpec((d, tn), lambda i, j: (0, j)),
                  pl.BlockSpec((tm, tn), lambda i, j: (i, j)),
                  pl.BlockSpec((1, d), lambda i, j: (0, 0))],
        out_specs=pl.BlockSpec((tm, d), lambda i, j: (i, 0)),
        out_shape=jax.ShapeDtypeStruct((s, d), F32),
        compiler_params=_params("parallel", "arbitrary"),
        name="out_proj_norm",
    )(merged, w, x, g.reshape(1, d))


def kernel(x, mem, norm_in, norm_mem, w_in, w_pool_group, pool_scale, w_mem_k, w_mem_v,
           w_proj_pool, w_proj_ret, w_proj_mem, w_out, norm_f):
    b, s, d = x.shape
    depth = w_in.shape[0]
    assert b == 1 and d == D_MODEL and depth == 1

    pos = jnp.arange(s, dtype=F32)
    inv = ROPE_BASE ** (-jnp.arange(ROPE_HALF, dtype=F32) / ROPE_HALF)
    ang = pos[:, None] * inv[None, :]
    cos, sin = jnp.cos(ang), jnp.sin(ang)
    lg = jnp.log1p(-(2.0 ** (-5.0 - jnp.arange(RET_HEADS, dtype=F32))))
    lg_tab = jnp.broadcast_to(lg[:, None, None], (RET_HEADS, 8, 128))

    xs = x[0]
    mems = mem[0]
    h = _rmsnorm(xs, norm_in[0], tm=256)
    gd = POOL_GROUP_DIM
    tm_in = 2048
    first = dict(tm=tm_in, tn=512, row_blocks=1)
    rest = dict(tm=tm_in, tn=1024, row_blocks=s // tm_in - 1, first_row_block=1)
    z_lin, wb_lin, wb_mem_v = _in_proj(h, w_in[0], LIN_SEGS, "linear", None, None, **first,
                                       casts=(_Cast(w_mem_v[0], D_MODEL),))
    z_lin, w_cat = _in_proj(h, wb_lin, LIN_SEGS, "linear", None, None, **rest, z=z_lin,
                            casts=(_Cast(w_proj_ret[0], ACT_WIDTH, ACT_RET),))
    z_silu, wb_silu = _in_proj(h, w_in[0], SILU_SEGS, "silu", None, None, **first)
    z_silu, w_cat, wb_mem_k = _in_proj(h, wb_silu, SILU_SEGS, "silu", None, None, **rest, z=z_silu,
                                       casts=(_Cast(w_proj_pool[0], ACT_WIDTH, ACT_POOL, w_cat),
                                              _Cast(w_mem_k[0], D_MODEL)))
    z_sig, wb_sig = _in_proj(h, w_in[0], SIG_SEGS, "sigmoid", None, None, **first)
    z_sig, w_cat, wb_out = _in_proj(h, wb_sig, SIG_SEGS, "sigmoid", None, None, **rest, z=z_sig,
                                    casts=(_Cast(w_proj_mem[0], ACT_WIDTH, ACT_MEM, w_cat),
                                           _Cast(w_out[0], D_MODEL)))
    z_qk, wb_qk = _in_proj(h, w_in[0], ROPE_SEGS, "rope", cos, sin, **first)
    z_qk, wb_group = _in_proj(h, wb_qk, ROPE_SEGS, "rope", cos, sin, **rest, z=z_qk,
                              casts=(_Cast(w_pool_group[0].reshape(POOL_GROUPS * gd, gd), POOL_GROUPS * gd),))

    acts = _pool_branch(z_lin, z_silu, wb_group.reshape(POOL_GROUPS, gd, gd), pool_scale[0], tm=1024)
    acts = _retention_branch(z_qk, z_lin, z_silu, lg_tab, acts, chunk=256, n_chunks=8)

    memn = _rmsnorm(mems, norm_mem[0], tm=256)
    km = _matmul(memn, wb_mem_k, tm=256, tn=1024)
    vm = _matmul(memn, wb_mem_v, tm=256, tn=1024)
    acts = _mem_branch(z_lin, z_silu, km, vm, acts, tm=1024)

    merged = _merge_branches(acts, w_cat, z_sig, tm=1024, tn=1024)
    out = _out_proj(merged, wb_out, xs, norm_f, 512, 1024)
    return out[None]
```

```python
import functools
from typing import NamedTuple, Optional

import jax
import jax.numpy as jnp
from jax import lax
from jax.experimental import pallas as pl
from jax.experimental.pallas import tpu as pltpu

D_MODEL = 4096
POOL_WINDOWS = (2, 4, 8, 16)
POOL_GROUPS = 4
POOL_GROUP_DIM = D_MODEL // POOL_GROUPS
POOL_HALO = 32
RET_QK_DIM = 256
RET_HEADS = D_MODEL // RET_QK_DIM
RET_V_DIM = 2 * RET_QK_DIM
RET_V_WIDTH = RET_HEADS * RET_V_DIM
MEM_HEADS = 4
MEM_HEAD_DIM = D_MODEL // MEM_HEADS
ROPE_BASE = 10000.0
ROPE_HALF = RET_QK_DIM // 2
NORM_EPS = 1e-6

SEG_U_POOL, SEG_G_POOL, SEG_Q, SEG_K, SEG_V, SEG_G_RET, SEG_Q_MEM, SEG_G_MEM, SEG_A_POOL, SEG_A_RET, SEG_A_MEM = (
    0, 1, 2, 3, 4, 6, 8, 9, 10, 11, 12)
LIN_SEGS = (SEG_U_POOL, SEG_V, SEG_V + 1, SEG_Q_MEM)
LIN_U_POOL, LIN_V, LIN_Q_MEM = 0, 1, 3
SILU_SEGS = (SEG_G_POOL, SEG_G_RET, SEG_G_RET + 1, SEG_G_MEM)
SILU_G_POOL, SILU_G_RET, SILU_G_MEM = 0, 1, 3
SIG_SEGS = (SEG_A_POOL, SEG_A_RET, SEG_A_MEM)
SIG_A_POOL, SIG_A_RET, SIG_A_MEM = 0, 1, 2
ROPE_SEGS = (SEG_Q, SEG_K)
ROPE_Q, ROPE_K = 0, 1
ACT_POOL, ACT_RET, ACT_MEM = 0, D_MODEL, D_MODEL + RET_V_WIDTH
ACT_WIDTH = 2 * D_MODEL + RET_V_WIDTH

F32 = jnp.float32
BF16 = jnp.bfloat16
VMEM_LIMIT = 60 * 1024 * 1024


def _params(*sem):
    return pltpu.CompilerParams(dimension_semantics=sem, vmem_limit_bytes=VMEM_LIMIT)


def _sigmoid(x):
    return 0.5 * jnp.tanh(0.5 * x) + 0.5


def _rmsnorm_kernel(x_ref, g_ref, o_ref):
    x = x_ref[...]
    y = x * lax.rsqrt(jnp.mean(x * x, axis=-1, keepdims=True) + NORM_EPS)
    o_ref[...] = (y * g_ref[...]).astype(o_ref.dtype)


def _rmsnorm(x, g, tm):
    rows, d = x.shape
    return pl.pallas_call(
        _rmsnorm_kernel,
        grid=(rows // tm,),
        in_specs=[pl.BlockSpec((tm, d), lambda i: (i, 0)),
                  pl.BlockSpec((1, d), lambda i: (0, 0))],
        out_specs=pl.BlockSpec((tm, d), lambda i: (i, 0)),
        out_shape=jax.ShapeDtypeStruct((rows, d), BF16),
        compiler_params=_params("parallel"),
        name="rmsnorm",
    )(x, g.reshape(1, d))


class _Cast(NamedTuple):
    src: jax.Array
    dst_rows: int
    dst_offset: int = 0
    dst: Optional[jax.Array] = None


def _in_proj_kernel(*refs, epilogue, tn, sub_rows, n_casts, n_alias, emit_w):
    h_ref, w_ref = refs[:2]
    n_in = 2 + (2 if epilogue == "rope" else 0)
    cast_in = refs[n_in:n_in + n_casts]
    outs = refs[n_in + n_casts + n_alias:]
    o_ref = outs[0]
    cast_out = outs[1 + emit_w:]
    if emit_w:
        outs[1][...] = w_ref[...].astype(BF16)
        w_ref = outs[1]
    for r in range(h_ref.shape[0] // sub_rows):
        rows = slice(r * sub_rows, (r + 1) * sub_rows)
        acc = jnp.dot(h_ref[rows, :], w_ref[...], preferred_element_type=F32)
        if epilogue == "linear":
            o_ref[rows, :] = acc.astype(o_ref.dtype)
        elif epilogue == "silu":
            o_ref[rows, :] = (acc * _sigmoid(acc)).astype(o_ref.dtype)
        elif epilogue == "sigmoid":
            o_ref[rows, :] = _sigmoid(acc).astype(o_ref.dtype)
        else:
            assert epilogue == "rope"
            cos = refs[2][rows, :]
            sin = refs[3][rows, :]
            for hh in range(tn // RET_QK_DIM):
                lo = hh * RET_QK_DIM
                x1 = acc[:, lo:lo + ROPE_HALF]
                x2 = acc[:, lo + ROPE_HALF:lo + RET_QK_DIM]
                o_ref[rows, lo:lo + ROPE_HALF] = (x1 * cos - x2 * sin).astype(o_ref.dtype)
                o_ref[rows, lo + ROPE_HALF:lo + RET_QK_DIM] = (x2 * cos + x1 * sin).astype(o_ref.dtype)
    for src_ref, dst_ref in zip(cast_in, cast_out):
        dst_ref[...] = src_ref[...].astype(dst_ref.dtype)


def _in_proj(h, w, segs, epilogue, cos, sin, *, tm, tn, row_blocks, first_row_block=0, z=None, casts=(),
             sub_rows=256):
    s, d = h.shape
    nb = D_MODEL // tn
    nj = len(segs) * nb
    steps = row_blocks * nj
    emit_w = w.dtype == F32

    def w_block(j):
        if not emit_w:
            return j
        js = j // nb
        seg = segs[-1]
        for idx in range(len(segs) - 2, -1, -1):
            seg = jnp.where(js == idx, segs[idx], seg)
        return seg * nb + j % nb

    def row_block(i):
        return first_row_block + i

    in_specs = [pl.BlockSpec((tm, d), lambda i, j: (row_block(i), 0), pipeline_mode=pl.Buffered(1)),
                pl.BlockSpec((d, tn), lambda i, j: (0, w_block(j)))]
    args = [h, w]
    if epilogue == "rope":
        in_specs += [pl.BlockSpec((tm, ROPE_HALF), lambda i, j: (row_block(i), 0))] * 2
        args += [cos, sin]
    out_specs = [pl.BlockSpec((tm, tn), lambda i, j: (row_block(i), j))]
    out_shape = [jax.ShapeDtypeStruct((s, len(segs) * D_MODEL), BF16)]
    if emit_w:
        out_specs.append(pl.BlockSpec((d, tn), lambda i, j: (0, j)))
        out_shape.append(jax.ShapeDtypeStruct((d, len(segs) * D_MODEL), BF16))
    for c in casts:
        n_slabs = 1 << (steps.bit_length() - 1)
        slab_rows, cols = c.src.shape[0] // n_slabs, c.src.shape[1]
        first_slab = c.dst_offset // slab_rows

        def slab(i, j, n_slabs=n_slabs):
            return jnp.minimum(i * nj + j, n_slabs - 1)

        in_specs.append(pl.BlockSpec((slab_rows, cols), lambda i, j, slab=slab: (slab(i, j), 0)))
        out_specs.append(pl.BlockSpec((slab_rows, cols),
                                      lambda i, j, slab=slab, first_slab=first_slab: (first_slab + slab(i, j), 0)))
        out_shape.append(jax.ShapeDtypeStruct((c.dst_rows, cols), BF16))
        args.append(c.src)
    n_cast_args = len(args)
    aliases = {}
    if z is not None:
        aliases[len(args)] = 0
        in_specs.append(pl.BlockSpec(memory_space=pl.ANY))
        args.append(z)
    for out_idx, c in enumerate(casts, start=1 + emit_w):
        if c.dst is not None:
            aliases[len(args)] = out_idx
            in_specs.append(pl.BlockSpec(memory_space=pl.ANY))
            args.append(c.dst)
    return pl.pallas_call(
        functools.partial(_in_proj_kernel, epilogue=epilogue, tn=tn, sub_rows=sub_rows, n_casts=len(casts),
                          n_alias=len(args) - n_cast_args, emit_w=emit_w),
        grid=(row_blocks, nj),
        in_specs=in_specs,
        out_specs=out_specs,
        out_shape=out_shape,
        input_output_aliases=aliases,
        compiler_params=_params("parallel", "arbitrary"),
        name="in_proj_" + epilogue + ("_first" if emit_w else "_rest"),
    )(*args)


def _matmul_kernel(a_ref, b_ref, o_ref):
    o_ref[...] = jnp.dot(a_ref[...], b_ref[...], preferred_element_type=F32).astype(o_ref.dtype)


def _matmul(a, b, tm, tn):
    m, k = a.shape
    n = b.shape[1]
    return pl.pallas_call(
        _matmul_kernel,
        grid=(m // tm, n // tn),
        in_specs=[pl.BlockSpec((tm, k), lambda i, j: (i, 0)),
                  pl.BlockSpec((k, tn), lambda i, j: (0, j))],
        out_specs=pl.BlockSpec((tm, tn), lambda i, j: (i, j)),
        out_shape=jax.ShapeDtypeStruct((m, n), BF16),
        compiler_params=_params("parallel", "arbitrary"),
        name="mem_kv_proj",
    )(a, b)


def _pool_kernel(u_ref, halo_ref, gate_ref, w_ref, scale_ref, o_ref, buf_a, buf_b, *, tm):
    g = pl.program_id(0)
    i = pl.program_id(1)
    u = u_ref[...].astype(F32)
    halo = halo_ref[...].astype(F32)
    end = POOL_HALO + tm
    buf_a[0:POOL_HALO, :] = jnp.where(i == 0, 0.0, halo)
    buf_a[POOL_HALO:end, :] = u
    t = i * tm + lax.broadcasted_iota(jnp.int32, (tm, 1), 0)

    for gi, win in enumerate(POOL_WINDOWS):
        @pl.when(g == gi)
        def _(win=win):
            src, dst = buf_a, buf_b
            span, lo = 1, 8
            while span < win:
                dst[lo:end, :] = src[lo:end, :] + src[lo - span:end - span, :]
                src, dst = dst, src
                span, lo = 2 * span, lo + 8
            wsum = src[POOL_HALO:end, :]
            cnt = jnp.minimum(t + 1, win).astype(F32)
            mixed = wsum / cnt - u
            y = jnp.dot(mixed.astype(BF16), w_ref[0], preferred_element_type=F32)
            y = y * scale_ref[...] * gate_ref[...].astype(F32)
            o_ref[...] = y.astype(o_ref.dtype)


def _pool_branch(z_lin, z_silu, w_group, scale, tm):
    s = z_lin.shape[0]
    gd = POOL_GROUP_DIM
    halo_blocks_per_tile = tm // POOL_HALO
    return pl.pallas_call(
        functools.partial(_pool_kernel, tm=tm),
        grid=(POOL_GROUPS, s // tm),
        in_specs=[
            pl.BlockSpec((tm, gd), lambda g, i: (i, LIN_U_POOL * POOL_GROUPS + g)),
            pl.BlockSpec((POOL_HALO, gd),
                         lambda g, i: (jnp.maximum(i * halo_blocks_per_tile - 1, 0), LIN_U_POOL * POOL_GROUPS + g)),
            pl.BlockSpec((tm, gd), lambda g, i: (i, SILU_G_POOL * POOL_GROUPS + g)),
            pl.BlockSpec((1, gd, gd), lambda g, i: (g, 0, 0)),
            pl.BlockSpec((1, gd), lambda g, i: (0, g)),
        ],
        out_specs=pl.BlockSpec((tm, gd), lambda g, i: (i, ACT_POOL // gd + g)),
        out_shape=jax.ShapeDtypeStruct((s, ACT_WIDTH), BF16),
        scratch_shapes=[pltpu.VMEM((tm + POOL_HALO, gd), F32)] * 2,
        compiler_params=_params("parallel", "arbitrary"),
        name="pool_branch",
    )(z_lin, z_lin, z_silu, w_group, scale.reshape(1, D_MODEL))


def _retention_kernel(lg_ref, q_ref, k_ref, v_ref, gate_ref, acts_ref, o_ref, state_ref, *, chunk, n_chunks):
    del acts_ref
    @pl.when(pl.program_id(1) == 0)
    def _():
        state_ref[...] = jnp.zeros_like(state_ref)

    k_scale = RET_QK_DIM ** -0.5
    lg = lg_ref[0][0:1, 0:1]
    row = lax.broadcasted_iota(jnp.int32, (chunk, 1), 0).astype(F32)
    col = lax.broadcasted_iota(jnp.int32, (1, chunk), 1).astype(F32)
    diff = row - col
    intra = jnp.where(diff >= 0, jnp.exp(jnp.maximum(diff, 0.0) * lg) * k_scale, 0.0)
    q_decay = jnp.exp((row + 1.0) * lg)
    k_decay = jnp.exp((chunk - 1.0 - row) * lg) * k_scale
    chunk_decay = jnp.exp(chunk * lg)

    for c in range(n_chunks):
        rows = slice(c * chunk, (c + 1) * chunk)
        q = q_ref[rows, :]
        k = k_ref[rows, :]
        v = v_ref[rows, :]
        state = state_ref[...]
        scores = lax.dot_general(q, k, (((1,), (1,)), ((), ())), preferred_element_type=F32) * intra
        inner = jnp.dot(scores.astype(BF16), v, preferred_element_type=F32)
        cross = jnp.dot(q, state.astype(BF16), preferred_element_type=F32) * q_decay
        kd = (k.astype(F32) * k_decay).astype(BF16)
        state_ref[...] = state * chunk_decay + lax.dot_general(
            kd, v, (((0,), (0,)), ((), ())), preferred_element_type=F32)

        o = inner + cross
        o = o * lax.rsqrt(jnp.mean(o * o, axis=-1, keepdims=True) + NORM_EPS)
        o_ref[rows, :] = (o * gate_ref[rows, :].astype(F32)).astype(o_ref.dtype)


def _retention_branch(z_qk, z_lin, z_silu, lg_tab, acts, chunk, n_chunks):
    s = z_qk.shape[0]
    rows = chunk * n_chunks
    q_blk = ROPE_Q * D_MODEL // RET_QK_DIM
    k_blk = ROPE_K * D_MODEL // RET_QK_DIM
    v_blk = LIN_V * D_MODEL // RET_V_DIM
    g_blk = SILU_G_RET * D_MODEL // RET_V_DIM
    return pl.pallas_call(
        functools.partial(_retention_kernel, chunk=chunk, n_chunks=n_chunks),
        grid=(RET_HEADS, s // rows),
        in_specs=[
            pl.BlockSpec((1, 8, 128), lambda h, n: (h, 0, 0)),
            pl.BlockSpec((rows, RET_QK_DIM), lambda h, n: (n, q_blk + h)),
            pl.BlockSpec((rows, RET_QK_DIM), lambda h, n: (n, k_blk + h)),
            pl.BlockSpec((rows, RET_V_DIM), lambda h, n: (n, v_blk + h)),
            pl.BlockSpec((rows, RET_V_DIM), lambda h, n: (n, g_blk + h)),
            pl.BlockSpec(memory_space=pl.ANY),
        ],
        out_specs=pl.BlockSpec((rows, RET_V_DIM), lambda h, n: (n, ACT_RET // RET_V_DIM + h)),
        out_shape=jax.ShapeDtypeStruct(acts.shape, acts.dtype),
        input_output_aliases={5: 0},
        scratch_shapes=[pltpu.VMEM((RET_QK_DIM, RET_V_DIM), F32)],
        compiler_params=_params("parallel", "arbitrary"),
        name="retention_branch",
    )(lg_tab, z_qk, z_qk, z_lin, z_silu, acts)


def _mem_attn_kernel(q_ref, gate_ref, k_ref, v_ref, acts_ref, o_ref):
    del acts_ref
    scores = lax.dot_general(q_ref[...], k_ref[...], (((1,), (1,)), ((), ())),
                             preferred_element_type=F32) * (MEM_HEAD_DIM ** -0.5)
    e = jnp.exp(scores - jnp.max(scores, axis=-1, keepdims=True))
    p = e * (1.0 / jnp.sum(e, axis=-1, keepdims=True))
    mo = jnp.dot(p.astype(BF16), v_ref[...], preferred_element_type=F32)
    o_ref[...] = (mo * gate_ref[...].astype(F32)).astype(o_ref.dtype)


def _mem_branch(z_lin, z_silu, km, vm, acts, tm):
    s = z_lin.shape[0]
    m = km.shape[0]
    hd = MEM_HEAD_DIM
    return pl.pallas_call(
        _mem_attn_kernel,
        grid=(MEM_HEADS, s // tm),
        in_specs=[
            pl.BlockSpec((tm, hd), lambda h, i: (i, LIN_Q_MEM * MEM_HEADS + h)),
            pl.BlockSpec((tm, hd), lambda h, i: (i, SILU_G_MEM * MEM_HEADS + h)),
            pl.BlockSpec((m, hd), lambda h, i: (0, h)),
            pl.BlockSpec((m, hd), lambda h, i: (0, h)),
            pl.BlockSpec(memory_space=pl.ANY),
        ],
        out_specs=pl.BlockSpec((tm, hd), lambda h, i: (i, ACT_MEM // hd + h)),
        out_shape=jax.ShapeDtypeStruct(acts.shape, acts.dtype),
        input_output_aliases={4: 0},
        compiler_params=_params("parallel", "arbitrary"),
        name="mem_branch",
    )(z_lin, z_silu, km, vm, acts)


def _merge_kernel(a_ref, w_ref, gate_ref, o_ref, acc_ref, *, sub_rows):
    kb = pl.program_id(2)
    last = pl.num_programs(2) - 1

    def accumulate(first, final):
        w = w_ref[...]
        for r in range(a_ref.shape[0] // sub_rows):
            rows = slice(r * sub_rows, (r + 1) * sub_rows)
            acc = jnp.dot(a_ref[rows, :], w, preferred_element_type=F32) * gate_ref[rows, :].astype(F32)
            if not first:
                acc = acc + acc_ref[rows, :]
            if final:
                o_ref[rows, :] = acc.astype(o_ref.dtype)
            else:
                acc_ref[rows, :] = acc

    @pl.when(kb == 0)
    def _():
        accumulate(True, False)

    @pl.when((kb > 0) & (kb < last))
    def _():
        accumulate(False, False)

    @pl.when(kb == last)
    def _():
        accumulate(False, True)


def _merge_branches(acts, w_cat, z_sig, tm, tn):
    s = acts.shape[0]
    n_chunks = ACT_WIDTH // D_MODEL
    nb = D_MODEL // tn

    def gate_block(kb, j):
        return ((kb + 1) // 2) * nb + j

    return pl.pallas_call(
        functools.partial(_merge_kernel, sub_rows=256),
        grid=(s // tm, nb, n_chunks),
        in_specs=[pl.BlockSpec((tm, D_MODEL), lambda i, j, kb: (i, kb)),
                  pl.BlockSpec((D_MODEL, tn), lambda i, j, kb: (kb, j)),
                  pl.BlockSpec((tm, tn), lambda i, j, kb: (i, gate_block(kb, j)))],
        out_specs=pl.BlockSpec((tm, tn), lambda i, j, kb: (i, j)),
        out_shape=jax.ShapeDtypeStruct((s, D_MODEL), BF16),
        scratch_shapes=[pltpu.VMEM((tm, tn), F32)],
        compiler_params=_params("parallel", "parallel", "arbitrary"),
        name="merge_branches",
    )(acts, w_cat, z_sig)


def _out_proj_kernel(m_ref, w_ref, x_ref, g_ref, o_ref, *, tn):
    j = pl.program_id(1)
    col = pl.multiple_of(j * tn, tn)
    o_ref[:, pl.ds(col, tn)] = x_ref[...] + jnp.dot(m_ref[...], w_ref[...], preferred_element_type=F32)

    @pl.when(j == pl.num_programs(1) - 1)
    def _():
        y = o_ref[...]
        y = y * lax.rsqrt(jnp.mean(y * y, axis=-1, keepdims=True) + NORM_EPS)
        o_ref[...] = y * g_ref[...]


def _out_proj(merged, w, x, g, tm, tn):
    s, d = x.shape
    return pl.pallas_call(
        functools.partial(_out_proj_kernel, tn=tn),
        grid=(s // tm, d // tn),
        in_specs=[pl.BlockSpec((tm, d), lambda i, j: (i, 0)),
                  pl.BlockSpec((d, tn), lambda i, j: (0, j)),
                  pl.BlockSpec((tm, tn), lambda i, j: (i, j)),
                  pl.BlockSpec((1, d), lambda i, j: (0, 0))],
        out_specs=pl.BlockSpec((tm, d), lambda i, j: (i, 0)),
        out_shape=jax.ShapeDtypeStruct((s, d), F32),
        compiler_params=_params("parallel", "arbitrary"),
        name="out_proj_norm",
    )(merged, w, x, g.reshape(1, d))


def kernel(x, mem, norm_in, norm_mem, w_in, w_pool_group, pool_scale, w_mem_k, w_mem_v,
           w_proj_pool, w_proj_ret, w_proj_mem, w_out, norm_f):
    b, s, d = x.shape
    depth = w_in.shape[0]
    assert b == 1 and d == D_MODEL and depth == 1

    pos = jnp.arange(s, dtype=F32)
    inv = ROPE_BASE ** (-jnp.arange(ROPE_HALF, dtype=F32) / ROPE_HALF)
    ang = pos[:, None] * inv[None, :]
    cos, sin = jnp.cos(ang), jnp.sin(ang)
    lg = jnp.log1p(-(2.0 ** (-5.0 - jnp.arange(RET_HEADS, dtype=F32))))
    lg_tab = jnp.broadcast_to(lg[:, None, None], (RET_HEADS, 8, 128))

    xs = x[0]
    mems = mem[0]
    h = _rmsnorm(xs, norm_in[0], tm=512)
    gd = POOL_GROUP_DIM
    tm_in = 2048
    first = dict(tm=tm_in, tn=512, row_blocks=1)
    rest = dict(tm=tm_in, tn=1024, row_blocks=s // tm_in - 1, first_row_block=1)
    z_lin, wb_lin, wb_mem_v = _in_proj(h, w_in[0], LIN_SEGS, "linear", None, None, **first, sub_rows=512,
                                       casts=(_Cast(w_mem_v[0], D_MODEL),))
    z_lin, w_cat = _in_proj(h, wb_lin, LIN_SEGS, "linear", None, None, **rest, z=z_lin, sub_rows=512,
                            casts=(_Cast(w_proj_ret[0], ACT_WIDTH, ACT_RET),))
    z_silu, wb_silu = _in_proj(h, w_in[0], SILU_SEGS, "silu", None, None, **first)
    z_silu, w_cat, wb_mem_k = _in_proj(h, wb_silu, SILU_SEGS, "silu", None, None, **rest, z=z_silu,
                                       casts=(_Cast(w_proj_pool[0], ACT_WIDTH, ACT_POOL, w_cat),
                                              _Cast(w_mem_k[0], D_MODEL)))
    z_sig, wb_sig = _in_proj(h, w_in[0], SIG_SEGS, "sigmoid", None, None, **first, sub_rows=1024)
    z_sig, w_cat, wb_out = _in_proj(h, wb_sig, SIG_SEGS, "sigmoid", None, None, **rest, z=z_sig, sub_rows=512,
                                    casts=(_Cast(w_proj_mem[0], ACT_WIDTH, ACT_MEM, w_cat),
                                           _Cast(w_out[0], D_MODEL)))
    z_qk, wb_qk = _in_proj(h, w_in[0], ROPE_SEGS, "rope", cos, sin, **first, sub_rows=2048)
    z_qk, wb_group = _in_proj(h, wb_qk, ROPE_SEGS, "rope", cos, sin, **rest, z=z_qk, sub_rows=1024,
                              casts=(_Cast(w_pool_group[0].reshape(POOL_GROUPS * gd, gd), POOL_GROUPS * gd),))

    acts = _pool_branch(z_lin, z_silu, wb_group.reshape(POOL_GROUPS, gd, gd), pool_scale[0], tm=1024)
    acts = _retention_branch(z_qk, z_lin, z_silu, lg_tab, acts, chunk=256, n_chunks=8)

    memn = _rmsnorm(mems, norm_mem[0], tm=256)
    km = _matmul(memn, wb_mem_k, tm=256, tn=1024)
    vm = _matmul(memn, wb_mem_v, tm=256, tn=1024)
    acts = _mem_branch(z_lin, z_silu, km, vm, acts, tm=2048)

    merged = _merge_branches(acts, w_cat, z_sig, tm=1024, tn=1024)
    out = _out_proj(merged, wb_out, xs, norm_f, 512, 1024)
    return out[None]
```

```python
import functools
from typing import NamedTuple, Optional

import jax
import jax.numpy as jnp
from jax import lax
from jax.experimental import pallas as pl
from jax.experimental.pallas import tpu as pltpu

D_MODEL = 4096
POOL_WINDOWS = (2, 4, 8, 16)
POOL_GROUPS = 4
POOL_GROUP_DIM = D_MODEL // POOL_GROUPS
POOL_HALO = 32
RET_QK_DIM = 256
RET_HEADS = D_MODEL // RET_QK_DIM
RET_V_DIM = 2 * RET_QK_DIM
RET_V_WIDTH = RET_HEADS * RET_V_DIM
MEM_HEADS = 4
MEM_HEAD_DIM = D_MODEL // MEM_HEADS
ROPE_BASE = 10000.0
ROPE_HALF = RET_QK_DIM // 2
NORM_EPS = 1e-6

SEG_U_POOL, SEG_G_POOL, SEG_Q, SEG_K, SEG_V, SEG_G_RET, SEG_Q_MEM, SEG_G_MEM, SEG_A_POOL, SEG_A_RET, SEG_A_MEM = (
    0, 1, 2, 3, 4, 6, 8, 9, 10, 11, 12)
LIN_SEGS = (SEG_U_POOL, SEG_V, SEG_V + 1, SEG_Q_MEM)
LIN_U_POOL, LIN_V, LIN_Q_MEM = 0, 1, 3
SILU_SEGS = (SEG_G_POOL, SEG_G_RET, SEG_G_RET + 1, SEG_G_MEM)
SILU_G_POOL, SILU_G_RET, SILU_G_MEM = 0, 1, 3
SIG_SEGS = (SEG_A_POOL, SEG_A_RET, SEG_A_MEM)
SIG_A_POOL, SIG_A_RET, SIG_A_MEM = 0, 1, 2
ROPE_SEGS = (SEG_Q, SEG_K)
ROPE_Q, ROPE_K = 0, 1
ACT_POOL, ACT_RET, ACT_MEM = 0, D_MODEL, D_MODEL + RET_V_WIDTH
ACT_WIDTH = 2 * D_MODEL + RET_V_WIDTH

F32 = jnp.float32
BF16 = jnp.bfloat16
VMEM_LIMIT = 60 * 1024 * 1024
MXU_SUB_ROWS = 256


def _params(*sem):
    return pltpu.CompilerParams(dimension_semantics=sem, vmem_limit_bytes=VMEM_LIMIT)


def _sigmoid(x):
    return 0.5 * jnp.tanh(0.5 * x) + 0.5


def _silu(x):
    h = 0.5 * x
    return h + h * jnp.tanh(h)


def _row_chunks(total, size):
    chunks = [slice(a, a + size) for a in range(0, total - size, size)]
    half = size // 2
    return chunks + [slice(total - size, total - half), slice(total - half, total)]


def _rmsnorm_kernel(x_ref, g_ref, o_ref):
    x = x_ref[...]
    y = x * lax.rsqrt(jnp.mean(x * x, axis=-1, keepdims=True) + NORM_EPS)
    o_ref[...] = (y * g_ref[...]).astype(o_ref.dtype)


def _rmsnorm(x, g, tm):
    rows, d = x.shape
    return pl.pallas_call(
        _rmsnorm_kernel,
        grid=(rows // tm,),
        in_specs=[pl.BlockSpec((tm, d), lambda i: (i, 0)),
                  pl.BlockSpec((1, d), lambda i: (0, 0))],
        out_specs=pl.BlockSpec((tm, d), lambda i: (i, 0)),
        out_shape=jax.ShapeDtypeStruct((rows, d), BF16),
        compiler_params=_params("parallel"),
        name="rmsnorm",
    )(x, g.reshape(1, d))


class _Cast(NamedTuple):
    src: jax.Array
    dst_rows: int
    dst_offset: int = 0
    dst: Optional[jax.Array] = None


def _in_proj_kernel(*refs, epilogue, tn, sub_rows, n_casts, n_alias, emit_w):
    h_ref, w_ref = refs[:2]
    n_in = 2 + (2 if epilogue == "rope" else 0)
    cast_in = refs[n_in:n_in + n_casts]
    outs = refs[n_in + n_casts + n_alias:]
    o_ref = outs[0]
    cast_out = outs[1 + emit_w:]
    if emit_w:
        outs[1][...] = w_ref[...].astype(BF16)
        w_ref = outs[1]
    for rows in _row_chunks(h_ref.shape[0], sub_rows):
        acc = jnp.dot(h_ref[rows, :], w_ref[...], preferred_element_type=F32)
        if epilogue == "linear":
            o_ref[rows, :] = acc.astype(o_ref.dtype)
        elif epilogue == "silu":
            o_ref[rows, :] = _silu(acc).astype(o_ref.dtype)
        elif epilogue == "sigmoid":
            o_ref[rows, :] = _sigmoid(acc).astype(o_ref.dtype)
        else:
            assert epilogue == "rope"
            cos = refs[2][rows, :]
            sin = refs[3][rows, :]
            for hh in range(tn // RET_QK_DIM):
                lo = hh * RET_QK_DIM
                x1 = acc[:, lo:lo + ROPE_HALF]
                x2 = acc[:, lo + ROPE_HALF:lo + RET_QK_DIM]
                o_ref[rows, lo:lo + ROPE_HALF] = (x1 * cos - x2 * sin).astype(o_ref.dtype)
                o_ref[rows, lo + ROPE_HALF:lo + RET_QK_DIM] = (x2 * cos + x1 * sin).astype(o_ref.dtype)
    for src_ref, dst_ref in zip(cast_in, cast_out):
        dst_ref[...] = src_ref[...].astype(dst_ref.dtype)


def _in_proj(h, w, segs, epilogue, cos, sin, *, tm, tn, row_blocks, first_row_block=0, z=None, casts=()):
    s, d = h.shape
    nb = D_MODEL // tn
    nj = len(segs) * nb
    steps = row_blocks * nj
    emit_w = w.dtype == F32

    def w_block(j):
        if not emit_w:
            return j
        js = j // nb
        seg = segs[-1]
        for idx in range(len(segs) - 2, -1, -1):
            seg = jnp.where(js == idx, segs[idx], seg)
        return seg * nb + j % nb

    def row_block(i):
        return first_row_block + i

    in_specs = [pl.BlockSpec((tm, d), lambda i, j: (row_block(i), 0), pipeline_mode=pl.Buffered(1)),
                pl.BlockSpec((d, tn), lambda i, j: (0, w_block(j)))]
    args = [h, w]
    if epilogue == "rope":
        in_specs += [pl.BlockSpec((tm, ROPE_HALF), lambda i, j: (row_block(i), 0))] * 2
        args += [cos, sin]
    out_specs = [pl.BlockSpec((tm, tn), lambda i, j: (row_block(i), j))]
    out_shape = [jax.ShapeDtypeStruct((s, len(segs) * D_MODEL), BF16)]
    if emit_w:
        out_specs.append(pl.BlockSpec((d, tn), lambda i, j: (0, j)))
        out_shape.append(jax.ShapeDtypeStruct((d, len(segs) * D_MODEL), BF16))
    for c in casts:
        n_slabs = 1 << (steps.bit_length() - 1)
        slab_rows, cols = c.src.shape[0] // n_slabs, c.src.shape[1]
        first_slab = c.dst_offset // slab_rows

        def slab(i, j, n_slabs=n_slabs):
            return jnp.minimum(i * nj + j, n_slabs - 1)

        in_specs.append(pl.BlockSpec((slab_rows, cols), lambda i, j, slab=slab: (slab(i, j), 0)))
        out_specs.append(pl.BlockSpec((slab_rows, cols),
                                      lambda i, j, slab=slab, first_slab=first_slab: (first_slab + slab(i, j), 0)))
        out_shape.append(jax.ShapeDtypeStruct((c.dst_rows, cols), BF16))
        args.append(c.src)
    n_cast_args = len(args)
    aliases = {}
    if z is not None:
        aliases[len(args)] = 0
        in_specs.append(pl.BlockSpec(memory_space=pl.ANY))
        args.append(z)
    for out_idx, c in enumerate(casts, start=1 + emit_w):
        if c.dst is not None:
            aliases[len(args)] = out_idx
            in_specs.append(pl.BlockSpec(memory_space=pl.ANY))
            args.append(c.dst)
    return pl.pallas_call(
        functools.partial(_in_proj_kernel, epilogue=epilogue, tn=tn, sub_rows=MXU_SUB_ROWS, n_casts=len(casts),
                          n_alias=len(args) - n_cast_args, emit_w=emit_w),
        grid=(row_blocks, nj),
        in_specs=in_specs,
        out_specs=out_specs,
        out_shape=out_shape,
        input_output_aliases=aliases,
        compiler_params=_params("parallel", "arbitrary"),
        name="in_proj_" + epilogue + ("_first" if emit_w else "_rest"),
    )(*args)


def _matmul_kernel(a_ref, b_ref, o_ref):
    o_ref[...] = jnp.dot(a_ref[...], b_ref[...], preferred_element_type=F32).astype(o_ref.dtype)


def _matmul(a, b, tm, tn):
    m, k = a.shape
    n = b.shape[1]
    return pl.pallas_call(
        _matmul_kernel,
        grid=(m // tm, n // tn),
        in_specs=[pl.BlockSpec((tm, k), lambda i, j: (i, 0)),
                  pl.BlockSpec((k, tn), lambda i, j: (0, j))],
        out_specs=pl.BlockSpec((tm, tn), lambda i, j: (i, j)),
        out_shape=jax.ShapeDtypeStruct((m, n), BF16),
        compiler_params=_params("parallel", "arbitrary"),
        name="mem_kv_proj",
    )(a, b)


def _pool_kernel(u_ref, halo_ref, gate_ref, w_ref, scale_ref, o_ref, buf_a, buf_b, *, tm):
    g = pl.program_id(0)
    i = pl.program_id(1)
    u = u_ref[...].astype(F32)
    halo = halo_ref[...].astype(F32)
    end = POOL_HALO + tm
    buf_a[0:POOL_HALO, :] = jnp.where(i == 0, 0.0, halo)
    buf_a[POOL_HALO:end, :] = u
    t = i * tm + lax.broadcasted_iota(jnp.int32, (tm, 1), 0)

    for gi, win in enumerate(POOL_WINDOWS):
        @pl.when(g == gi)
        def _(win=win):
            src, dst = buf_a, buf_b
            span, lo = 1, 8
            while span < win:
                dst[lo:end, :] = src[lo:end, :] + src[lo - span:end - span, :]
                src, dst = dst, src
                span, lo = 2 * span, lo + 8
            wsum = src[POOL_HALO:end, :]
            cnt = jnp.minimum(t + 1, win).astype(F32)
            mixed = wsum / cnt - u
            y = jnp.dot(mixed.astype(BF16), w_ref[0], preferred_element_type=F32)
            y = y * scale_ref[...] * gate_ref[...].astype(F32)
            o_ref[...] = y.astype(o_ref.dtype)


def _pool_branch(z_lin, z_silu, w_group, scale, tm):
    s = z_lin.shape[0]
    gd = POOL_GROUP_DIM
    halo_blocks_per_tile = tm // POOL_HALO
    return pl.pallas_call(
        functools.partial(_pool_kernel, tm=tm),
        grid=(POOL_GROUPS, s // tm),
        in_specs=[
            pl.BlockSpec((tm, gd), lambda g, i: (i, LIN_U_POOL * POOL_GROUPS + g)),
            pl.BlockSpec((POOL_HALO, gd),
                         lambda g, i: (jnp.maximum(i * halo_blocks_per_tile - 1, 0), LIN_U_POOL * POOL_GROUPS + g)),
            pl.BlockSpec((tm, gd), lambda g, i: (i, SILU_G_POOL * POOL_GROUPS + g)),
            pl.BlockSpec((1, gd, gd), lambda g, i: (g, 0, 0)),
            pl.BlockSpec((1, gd), lambda g, i: (0, g)),
        ],
        out_specs=pl.BlockSpec((tm, gd), lambda g, i: (i, ACT_POOL // gd + g)),
        out_shape=jax.ShapeDtypeStruct((s, ACT_WIDTH), BF16),
        scratch_shapes=[pltpu.VMEM((tm + POOL_HALO, gd), F32)] * 2,
        compiler_params=_params("parallel", "arbitrary"),
        name="pool_branch",
    )(z_lin, z_lin, z_silu, w_group, scale.reshape(1, D_MODEL))


def _retention_kernel(lg_ref, q_ref, k_ref, v_ref, gate_ref, acts_ref, o_ref, state_ref, *, chunk, n_chunks):
    del acts_ref
    @pl.when(pl.program_id(1) == 0)
    def _():
        state_ref[...] = jnp.zeros_like(state_ref)

    k_scale = RET_QK_DIM ** -0.5
    lg = lg_ref[0][0:1, 0:1]
    row = lax.broadcasted_iota(jnp.int32, (chunk, 1), 0).astype(F32)
    col = lax.broadcasted_iota(jnp.int32, (1, chunk), 1).astype(F32)
    diff = row - col
    intra = jnp.where(diff >= 0, jnp.exp(jnp.maximum(diff, 0.0) * lg) * k_scale, 0.0)
    q_decay = jnp.exp((row + 1.0) * lg)
    k_decay = jnp.exp((chunk - 1.0 - row) * lg) * k_scale
    chunk_decay = jnp.exp(chunk * lg)

    for c in range(n_chunks):
        rows = slice(c * chunk, (c + 1) * chunk)
        q = q_ref[rows, :]
        k = k_ref[rows, :]
        v = v_ref[rows, :]
        state = state_ref[...]
        scores = lax.dot_general(q, k, (((1,), (1,)), ((), ())), preferred_element_type=F32) * intra
        inner = jnp.dot(scores.astype(BF16), v, preferred_element_type=F32)
        cross = jnp.dot(q, state.astype(BF16), preferred_element_type=F32) * q_decay
        kd = (k.astype(F32) * k_decay).astype(BF16)
        state_ref[...] = state * chunk_decay + lax.dot_general(
            kd, v, (((0,), (0,)), ((), ())), preferred_element_type=F32)

        o = inner + cross
        o = o * lax.rsqrt(jnp.mean(o * o, axis=-1, keepdims=True) + NORM_EPS)
        o_ref[rows, :] = (o * gate_ref[rows, :].astype(F32)).astype(o_ref.dtype)


def _retention_branch(z_qk, z_lin, z_silu, lg_tab, acts, chunk, n_chunks):
    s = z_qk.shape[0]
    rows = chunk * n_chunks
    q_blk = ROPE_Q * D_MODEL // RET_QK_DIM
    k_blk = ROPE_K * D_MODEL // RET_QK_DIM
    v_blk = LIN_V * D_MODEL // RET_V_DIM
    g_blk = SILU_G_RET * D_MODEL // RET_V_DIM
    return pl.pallas_call(
        functools.partial(_retention_kernel, chunk=chunk, n_chunks=n_chunks),
        grid=(RET_HEADS, s // rows),
        in_specs=[
            pl.BlockSpec((1, 8, 128), lambda h, n: (h, 0, 0)),
            pl.BlockSpec((rows, RET_QK_DIM), lambda h, n: (n, q_blk + h)),
            pl.BlockSpec((rows, RET_QK_DIM), lambda h, n: (n, k_blk + h)),
            pl.BlockSpec((rows, RET_V_DIM), lambda h, n: (n, v_blk + h)),
            pl.BlockSpec((rows, RET_V_DIM), lambda h, n: (n, g_blk + h)),
            pl.BlockSpec(memory_space=pl.ANY),
        ],
        out_specs=pl.BlockSpec((rows, RET_V_DIM), lambda h, n: (n, ACT_RET // RET_V_DIM + h)),
        out_shape=jax.ShapeDtypeStruct(acts.shape, acts.dtype),
        input_output_aliases={5: 0},
        scratch_shapes=[pltpu.VMEM((RET_QK_DIM, RET_V_DIM), F32)],
        compiler_params=_params("parallel", "arbitrary"),
        name="retention_branch",
    )(lg_tab, z_qk, z_qk, z_lin, z_silu, acts)


def _mem_attn_kernel(q_ref, gate_ref, k_ref, v_ref, acts_ref, o_ref):
    del acts_ref
    scores = lax.dot_general(q_ref[...], k_ref[...], (((1,), (1,)), ((), ())),
                             preferred_element_type=F32) * (MEM_HEAD_DIM ** -0.5)
    e = jnp.exp(scores - jnp.max(scores, axis=-1, keepdims=True))
    p = e * (1.0 / jnp.sum(e, axis=-1, keepdims=True))
    mo = jnp.dot(p.astype(BF16), v_ref[...], preferred_element_type=F32)
    o_ref[...] = (mo * gate_ref[...].astype(F32)).astype(o_ref.dtype)


def _mem_branch(z_lin, z_silu, km, vm, acts, tm):
    s = z_lin.shape[0]
    m = km.shape[0]
    hd = MEM_HEAD_DIM
    return pl.pallas_call(
        _mem_attn_kernel,
        grid=(MEM_HEADS, s // tm),
        in_specs=[
            pl.BlockSpec((tm, hd), lambda h, i: (i, LIN_Q_MEM * MEM_HEADS + h)),
            pl.BlockSpec((tm, hd), lambda h, i: (i, SILU_G_MEM * MEM_HEADS + h)),
            pl.BlockSpec((m, hd), lambda h, i: (0, h)),
            pl.BlockSpec((m, hd), lambda h, i: (0, h)),
            pl.BlockSpec(memory_space=pl.ANY),
        ],
        out_specs=pl.BlockSpec((tm, hd), lambda h, i: (i, ACT_MEM // hd + h)),
        out_shape=jax.ShapeDtypeStruct(acts.shape, acts.dtype),
        input_output_aliases={4: 0},
        compiler_params=_params("parallel", "arbitrary"),
        name="mem_branch",
    )(z_lin, z_silu, km, vm, acts)


def _merge_kernel(a_ref, w_ref, gate_ref, o_ref, acc_ref, *, sub_rows):
    kb = pl.program_id(2)
    last = pl.num_programs(2) - 1

    def accumulate(first, final):
        w = w_ref[...]
        for rows in _row_chunks(a_ref.shape[0], sub_rows):
            acc = jnp.dot(a_ref[rows, :], w, preferred_element_type=F32) * gate_ref[rows, :].astype(F32)
            if not first:
                acc = acc + acc_ref[rows, :]
            if final:
                o_ref[rows, :] = acc.astype(o_ref.dtype)
            else:
                acc_ref[rows, :] = acc

    @pl.when(kb == 0)
    def _():
        accumulate(True, False)

    @pl.when((kb > 0) & (kb < last))
    def _():
        accumulate(False, False)

    @pl.when(kb == last)
    def _():
        accumulate(False, True)


def _merge_branches(acts, w_cat, z_sig, tm, tn):
    s = acts.shape[0]
    n_chunks = ACT_WIDTH // D_MODEL
    nb = D_MODEL // tn

    def gate_block(kb, j):
        return ((kb + 1) // 2) * nb + j

    return pl.pallas_call(
        functools.partial(_merge_kernel, sub_rows=MXU_SUB_ROWS),
        grid=(s // tm, nb, n_chunks),
        in_specs=[pl.BlockSpec((tm, D_MODEL), lambda i, j, kb: (i, kb)),
                  pl.BlockSpec((D_MODEL, tn), lambda i, j, kb: (kb, j)),
                  pl.BlockSpec((tm, tn), lambda i, j, kb: (i, gate_block(kb, j)))],
        out_specs=pl.BlockSpec((tm, tn), lambda i, j, kb: (i, j)),
        out_shape=jax.ShapeDtypeStruct((s, D_MODEL), BF16),
        scratch_shapes=[pltpu.VMEM((tm, tn), F32)],
        compiler_params=_params("parallel", "parallel", "arbitrary"),
        name="merge_branches",
    )(acts, w_cat, z_sig)


def _out_proj_kernel(m_ref, w_ref, x_ref, g_ref, o_ref, *, sub_rows):
    for r in range(m_ref.shape[0] // sub_rows):
        rows = slice(r * sub_rows, (r + 1) * sub_rows)
        y = x_ref[rows, :] + jnp.dot(m_ref[rows, :], w_ref[...], preferred_element_type=F32)
        y = y * lax.rsqrt(jnp.mean(y * y, axis=-1, keepdims=True) + NORM_EPS)
        o_ref[rows, :] = y * g_ref[...]


def _out_proj(merged, w, x, g, tm):
    s, d = x.shape
    return pl.pallas_call(
        functools.partial(_out_proj_kernel, sub_rows=128),
        grid=(s // tm,),
        in_specs=[pl.BlockSpec((tm, d), lambda i: (i, 0)),
                  pl.BlockSpec((d, d), lambda i: (0, 0), pipeline_mode=pl.Buffered(1)),
                  pl.BlockSpec((tm, d), lambda i: (i, 0)),
                  pl.BlockSpec((1, d), lambda i: (0, 0))],
        out_specs=pl.BlockSpec((tm, d), lambda i: (i, 0)),
        out_shape=jax.ShapeDtypeStruct((s, d), F32),
        compiler_params=_params("arbitrary"),
        name="out_proj_norm",
    )(merged, w, x, g.reshape(1, d))


def kernel(x, mem, norm_in, norm_mem, w_in, w_pool_group, pool_scale, w_mem_k, w_mem_v,
           w_proj_pool, w_proj_ret, w_proj_mem, w_out, norm_f):
    b, s, d = x.shape
    depth = w_in.shape[0]
    assert b == 1 and d == D_MODEL and depth == 1

    pos = jnp.arange(s, dtype=F32)
    inv = ROPE_BASE ** (-jnp.arange(ROPE_HALF, dtype=F32) / ROPE_HALF)
    ang = pos[:, None] * inv[None, :]
    cos, sin = jnp.cos(ang), jnp.sin(ang)
    lg = jnp.log1p(-(2.0 ** (-5.0 - jnp.arange(RET_HEADS, dtype=F32))))
    lg_tab = jnp.broadcast_to(lg[:, None, None], (RET_HEADS, 8, 128))

    xs = x[0]
    mems = mem[0]
    h = _rmsnorm(xs, norm_in[0], tm=512)
    gd = POOL_GROUP_DIM
    tm_in = 2048
    first = dict(tm=tm_in, tn=512, row_blocks=1)
    rest = dict(tm=tm_in, tn=1024, row_blocks=s // tm_in - 1, first_row_block=1)
    z_lin, wb_lin, wb_mem_v = _in_proj(h, w_in[0], LIN_SEGS, "linear", None, None, **first,
                                       casts=(_Cast(w_mem_v[0], D_MODEL),))
    z_lin, w_cat = _in_proj(h, wb_lin, LIN_SEGS, "linear", None, None, **rest, z=z_lin,
                            casts=(_Cast(w_proj_ret[0], ACT_WIDTH, ACT_RET),))
    z_silu, wb_silu = _in_proj(h, w_in[0], SILU_SEGS, "silu", None, None, **first)
    z_silu, w_cat, wb_mem_k = _in_proj(h, wb_silu, SILU_SEGS, "silu", None, None, **rest, z=z_silu,
                                       casts=(_Cast(w_proj_pool[0], ACT_WIDTH, ACT_POOL, w_cat),
                                              _Cast(w_mem_k[0], D_MODEL)))
    z_sig, wb_sig = _in_proj(h, w_in[0], SIG_SEGS, "sigmoid", None, None, **first)
    z_sig, w_cat, wb_out = _in_proj(h, wb_sig, SIG_SEGS, "sigmoid", None, None, **rest, z=z_sig,
                                    casts=(_Cast(w_proj_mem[0], ACT_WIDTH, ACT_MEM, w_cat),
                                           _Cast(w_out[0], D_MODEL)))
    z_qk, wb_qk = _in_proj(h, w_in[0], ROPE_SEGS, "rope", cos, sin, **first)
    z_qk, wb_group = _in_proj(h, wb_qk, ROPE_SEGS, "rope", cos, sin, **rest, z=z_qk,
                              casts=(_Cast(w_pool_group[0].reshape(POOL_GROUPS * gd, gd), POOL_GROUPS * gd),))

    acts = _pool_branch(z_lin, z_silu, wb_group.reshape(POOL_GROUPS, gd, gd), pool_scale[0], tm=1024)
    acts = _retention_branch(z_qk, z_lin, z_silu, lg_tab, acts, chunk=256, n_chunks=8)

    memn = _rmsnorm(mems, norm_mem[0], tm=256)
    km = _matmul(memn, wb_mem_k, tm=256, tn=1024)
    vm = _matmul(memn, wb_mem_v, tm=256, tn=1024)
    acts = _mem_branch(z_lin, z_silu, km, vm, acts, tm=2048)

    merged = _merge_branches(acts, w_cat, z_sig, tm=1024, tn=1024)
    out = _out_proj(merged, wb_out, xs, norm_f, tm=256)
    return out[None]
```

```python
import functools
from typing import NamedTuple, Optional

import jax
import jax.numpy as jnp
from jax import lax
from jax.experimental import pallas as pl
from jax.experimental.pallas import tpu as pltpu

D_MODEL = 4096
POOL_WINDOWS = (2, 4, 8, 16)
POOL_GROUPS = 4
POOL_GROUP_DIM = D_MODEL // POOL_GROUPS
POOL_HALO = 32
RET_QK_DIM = 256
RET_HEADS = D_MODEL // RET_QK_DIM
RET_V_DIM = 2 * RET_QK_DIM
RET_V_WIDTH = RET_HEADS * RET_V_DIM
MEM_HEADS = 4
MEM_HEAD_DIM = D_MODEL // MEM_HEADS
ROPE_BASE = 10000.0
ROPE_HALF = RET_QK_DIM // 2
NORM_EPS = 1e-6

SEG_U_POOL, SEG_G_POOL, SEG_Q, SEG_K, SEG_V, SEG_G_RET, SEG_Q_MEM, SEG_G_MEM, SEG_A_POOL, SEG_A_RET, SEG_A_MEM = (
    0, 1, 2, 3, 4, 6, 8, 9, 10, 11, 12)
LIN_SEGS = (SEG_U_POOL, SEG_V, SEG_V + 1, SEG_Q_MEM)
LIN_U_POOL, LIN_V, LIN_Q_MEM = 0, 1, 3
SILU_SEGS = (SEG_G_POOL, SEG_G_RET, SEG_G_RET + 1, SEG_G_MEM)
SILU_G_POOL, SILU_G_RET, SILU_G_MEM = 0, 1, 3
SIG_SEGS = (SEG_A_POOL, SEG_A_RET, SEG_A_MEM)
SIG_A_POOL, SIG_A_RET, SIG_A_MEM = 0, 1, 2
ROPE_SEGS = (SEG_Q, SEG_K)
ROPE_Q, ROPE_K = 0, 1
ACT_POOL, ACT_RET, ACT_MEM = 0, D_MODEL, D_MODEL + RET_V_WIDTH
ACT_WIDTH = 2 * D_MODEL + RET_V_WIDTH

F32 = jnp.float32
BF16 = jnp.bfloat16
VMEM_LIMIT = 60 * 1024 * 1024
MXU_SUB_ROWS = 256


def _params(*sem):
    return pltpu.CompilerParams(dimension_semantics=sem, vmem_limit_bytes=VMEM_LIMIT)


def _sigmoid(x):
    return 0.5 * jnp.tanh(0.5 * x) + 0.5


def _silu(x):
    h = 0.5 * x
    return h + h * jnp.tanh(h)


def _row_chunks(total, size):
    chunks = [slice(a, a + size) for a in range(0, total - size, size)]
    half = size // 2
    return chunks + [slice(total - size, total - half), slice(total - half, total)]


def _rmsnorm_kernel(x_ref, g_ref, o_ref):
    x = x_ref[...]
    y = x * lax.rsqrt(jnp.mean(x * x, axis=-1, keepdims=True) + NORM_EPS)
    o_ref[...] = (y * g_ref[...]).astype(o_ref.dtype)


def _rmsnorm(x, g, tm, rows=None):
    d = x.shape[1]
    rows = rows or x.shape[0]
    return pl.pallas_call(
        _rmsnorm_kernel,
        grid=(rows // tm,),
        in_specs=[pl.BlockSpec((tm, d), lambda i: (i, 0)),
                  pl.BlockSpec((1, d), lambda i: (0, 0))],
        out_specs=pl.BlockSpec((tm, d), lambda i: (i, 0)),
        out_shape=jax.ShapeDtypeStruct((rows, d), BF16),
        compiler_params=_params("parallel"),
        name="rmsnorm",
    )(x, g.reshape(1, d))


class _Cast(NamedTuple):
    src: jax.Array
    dst_rows: int
    dst_offset: int = 0
    dst: Optional[jax.Array] = None
    gain: Optional[jax.Array] = None
    src_offset: int = 0
    rows: Optional[int] = None
    n_slabs: Optional[int] = None


def _in_proj_kernel(*refs, epilogue, tn, sub_rows, cast_has_gain, n_alias, emit_w):
    h_ref, w_ref = refs[:2]
    n_in = 2 + (2 if epilogue == "rope" else 0)
    cast_in, gains = [], []
    for has_gain in cast_has_gain:
        cast_in.append(refs[n_in])
        gains.append(refs[n_in + 1] if has_gain else None)
        n_in += 2 if has_gain else 1
    outs = refs[n_in + n_alias:]
    o_ref = outs[0]
    cast_out = outs[1 + emit_w:]
    if emit_w:
        outs[1][...] = w_ref[...].astype(BF16)
        w_ref = outs[1]
    for rows in _row_chunks(h_ref.shape[0], sub_rows):
        acc = jnp.dot(h_ref[rows, :], w_ref[...], preferred_element_type=F32)
        if epilogue == "linear":
            o_ref[rows, :] = acc.astype(o_ref.dtype)
        elif epilogue == "silu":
            o_ref[rows, :] = _silu(acc).astype(o_ref.dtype)
        elif epilogue == "sigmoid":
            o_ref[rows, :] = _sigmoid(acc).astype(o_ref.dtype)
        else:
            assert epilogue == "rope"
            cos = refs[2][rows, :]
            sin = refs[3][rows, :]
            for hh in range(tn // RET_QK_DIM):
                lo = hh * RET_QK_DIM
                x1 = acc[:, lo:lo + ROPE_HALF]
                x2 = acc[:, lo + ROPE_HALF:lo + RET_QK_DIM]
                o_ref[rows, lo:lo + ROPE_HALF] = (x1 * cos - x2 * sin).astype(o_ref.dtype)
                o_ref[rows, lo + ROPE_HALF:lo + RET_QK_DIM] = (x2 * cos + x1 * sin).astype(o_ref.dtype)
    for src_ref, gain_ref, dst_ref in zip(cast_in, gains, cast_out):
        v = src_ref[...]
        if gain_ref is not None:
            v = v * lax.rsqrt(jnp.mean(v * v, axis=-1, keepdims=True) + NORM_EPS) * gain_ref[...]
        dst_ref[...] = v.astype(dst_ref.dtype)


def _in_proj(h, w, segs, epilogue, cos, sin, *, s, tm, tn, row_blocks, first_row_block=0, z=None, casts=()):
    d = h.shape[1]
    assert h.shape[0] == row_blocks * tm
    nb = D_MODEL // tn
    nj = len(segs) * nb
    steps = row_blocks * nj
    emit_w = w.dtype == F32

    def w_block(j):
        if not emit_w:
            return j
        js = j // nb
        seg = segs[-1]
        for idx in range(len(segs) - 2, -1, -1):
            seg = jnp.where(js == idx, segs[idx], seg)
        return seg * nb + j % nb

    def row_block(i):
        return first_row_block + i

    in_specs = [pl.BlockSpec((tm, d), lambda i, j: (i, 0), pipeline_mode=pl.Buffered(1)),
                pl.BlockSpec((d, tn), lambda i, j: (0, w_block(j)))]
    args = [h, w]
    if epilogue == "rope":
        in_specs += [pl.BlockSpec((tm, ROPE_HALF), lambda i, j: (row_block(i), 0))] * 2
        args += [cos, sin]
    out_specs = [pl.BlockSpec((tm, tn), lambda i, j: (row_block(i), j))]
    out_shape = [jax.ShapeDtypeStruct((s, len(segs) * D_MODEL), BF16)]
    if emit_w:
        out_specs.append(pl.BlockSpec((d, tn), lambda i, j: (0, j)))
        out_shape.append(jax.ShapeDtypeStruct((d, len(segs) * D_MODEL), BF16))
    for c in casts:
        n_slabs = c.n_slabs or 1 << (steps.bit_length() - 1)
        assert n_slabs <= steps
        slab_rows, cols = (c.rows or c.src.shape[0]) // n_slabs, c.src.shape[1]
        src_first, dst_first = c.src_offset // slab_rows, c.dst_offset // slab_rows

        def slab(i, j, n_slabs=n_slabs):
            return jnp.minimum(i * nj + j, n_slabs - 1)

        in_specs.append(pl.BlockSpec((slab_rows, cols), lambda i, j, slab=slab, first=src_first: (first + slab(i, j), 0)))
        out_specs.append(pl.BlockSpec((slab_rows, cols), lambda i, j, slab=slab, first=dst_first: (first + slab(i, j), 0)))
        out_shape.append(jax.ShapeDtypeStruct((c.dst_rows, cols), BF16))
        args.append(c.src)
        if c.gain is not None:
            in_specs.append(pl.BlockSpec((1, cols), lambda i, j: (0, 0)))
            args.append(c.gain)
    n_cast_args = len(args)
    aliases = {}
    if z is not None:
        aliases[len(args)] = 0
        in_specs.append(pl.BlockSpec(memory_space=pl.ANY))
        args.append(z)
    for out_idx, c in enumerate(casts, start=1 + emit_w):
        if c.dst is not None:
            aliases[len(args)] = out_idx
            in_specs.append(pl.BlockSpec(memory_space=pl.ANY))
            args.append(c.dst)
    return pl.pallas_call(
        functools.partial(_in_proj_kernel, epilogue=epilogue, tn=tn, sub_rows=MXU_SUB_ROWS,
                          cast_has_gain=tuple(c.gain is not None for c in casts),
                          n_alias=len(args) - n_cast_args, emit_w=emit_w),
        grid=(row_blocks, nj),
        in_specs=in_specs,
        out_specs=out_specs,
        out_shape=out_shape,
        input_output_aliases=aliases,
        compiler_params=_params("parallel", "arbitrary"),
        name="in_proj_" + epilogue + ("_first" if emit_w else "_rest"),
    )(*args)


def _matmul_kernel(a_ref, b_ref, o_ref):
    o_ref[...] = jnp.dot(a_ref[...], b_ref[...], preferred_element_type=F32).astype(o_ref.dtype)


def _matmul(a, b, tm, tn):
    m, k = a.shape
    n = b.shape[1]
    return pl.pallas_call(
        _matmul_kernel,
        grid=(m // tm, n // tn),
        in_specs=[pl.BlockSpec((tm, k), lambda i, j: (i, 0)),
                  pl.BlockSpec((k, tn), lambda i, j: (0, j))],
        out_specs=pl.BlockSpec((tm, tn), lambda i, j: (i, j)),
        out_shape=jax.ShapeDtypeStruct((m, n), BF16),
        compiler_params=_params("parallel", "arbitrary"),
        name="mem_kv_proj",
    )(a, b)


def _pool_kernel(u_ref, halo_ref, gate_ref, w_ref, scale_ref, o_ref, buf_a, buf_b, *, tm):
    g = pl.program_id(0)
    i = pl.program_id(1)
    u = u_ref[...].astype(F32)
    halo = halo_ref[...].astype(F32)
    end = POOL_HALO + tm
    buf_a[0:POOL_HALO, :] = jnp.where(i == 0, 0.0, halo)
    buf_a[POOL_HALO:end, :] = u
    t = i * tm + lax.broadcasted_iota(jnp.int32, (tm, 1), 0)

    for gi, win in enumerate(POOL_WINDOWS):
        @pl.when(g == gi)
        def _(win=win):
            src, dst = buf_a, buf_b
            span, lo = 1, 8
            while span < win:
                dst[lo:end, :] = src[lo:end, :] + src[lo - span:end - span, :]
                src, dst = dst, src
                span, lo = 2 * span, lo + 8
            wsum = src[POOL_HALO:end, :]
            cnt = jnp.minimum(t + 1, win).astype(F32)
            mixed = wsum / cnt - u
            y = jnp.dot(mixed.astype(BF16), w_ref[0], preferred_element_type=F32)
            y = y * scale_ref[...] * gate_ref[...].astype(F32)
            o_ref[...] = y.astype(o_ref.dtype)


def _pool_branch(z_lin, z_silu, w_group, scale, tm):
    s = z_lin.shape[0]
    gd = POOL_GROUP_DIM
    halo_blocks_per_tile = tm // POOL_HALO
    return pl.pallas_call(
        functools.partial(_pool_kernel, tm=tm),
        grid=(POOL_GROUPS, s // tm),
        in_specs=[
            pl.BlockSpec((tm, gd), lambda g, i: (i, LIN_U_POOL * POOL_GROUPS + g)),
            pl.BlockSpec((POOL_HALO, gd),
                         lambda g, i: (jnp.maximum(i * halo_blocks_per_tile - 1, 0), LIN_U_POOL * POOL_GROUPS + g)),
            pl.BlockSpec((tm, gd), lambda g, i: (i, SILU_G_POOL * POOL_GROUPS + g)),
            pl.BlockSpec((1, gd, gd), lambda g, i: (g, 0, 0)),
            pl.BlockSpec((1, gd), lambda g, i: (0, g)),
        ],
        out_specs=pl.BlockSpec((tm, gd), lambda g, i: (i, ACT_POOL // gd + g)),
        out_shape=jax.ShapeDtypeStruct((s, ACT_WIDTH), BF16),
        scratch_shapes=[pltpu.VMEM((tm + POOL_HALO, gd), F32)] * 2,
        compiler_params=_params("parallel", "arbitrary"),
        name="pool_branch",
    )(z_lin, z_lin, z_silu, w_group, scale.reshape(1, D_MODEL))


def _retention_kernel(lg_ref, q_ref, k_ref, v_ref, gate_ref, acts_ref, o_ref, state_ref, *, chunk, n_chunks):
    del acts_ref
    @pl.when(pl.program_id(1) == 0)
    def _():
        state_ref[...] = jnp.zeros_like(state_ref)

    k_scale = RET_QK_DIM ** -0.5
    lg = lg_ref[0][0:1, 0:1]
    row = lax.broadcasted_iota(jnp.int32, (chunk, 1), 0).astype(F32)
    col = lax.broadcasted_iota(jnp.int32, (1, chunk), 1).astype(F32)
    diff = row - col
    intra = jnp.where(diff >= 0, jnp.exp(jnp.maximum(diff, 0.0) * lg) * k_scale, 0.0)
    q_decay = jnp.exp((row + 1.0) * lg)
    k_decay = jnp.exp((chunk - 1.0 - row) * lg) * k_scale
    chunk_decay = jnp.exp(chunk * lg)

    for c in range(n_chunks):
        rows = slice(c * chunk, (c + 1) * chunk)
        q = q_ref[rows, :]
        k = k_ref[rows, :]
        v = v_ref[rows, :]
        state = state_ref[...]
        scores = lax.dot_general(q, k, (((1,), (1,)), ((), ())), preferred_element_type=F32) * intra
        inner = jnp.dot(scores.astype(BF16), v, preferred_element_type=F32)
        cross = jnp.dot(q, state.astype(BF16), preferred_element_type=F32) * q_decay
        kd = (k.astype(F32) * k_decay).astype(BF16)
        state_ref[...] = state * chunk_decay + lax.dot_general(
            kd, v, (((0,), (0,)), ((), ())), preferred_element_type=F32)

        o = inner + cross
        o = o * lax.rsqrt(jnp.mean(o * o, axis=-1, keepdims=True) + NORM_EPS)
        o_ref[rows, :] = (o * gate_ref[rows, :].astype(F32)).astype(o_ref.dtype)


def _retention_branch(z_qk, z_lin, z_silu, lg_tab, acts, chunk, n_chunks):
    s = z_qk.shape[0]
    rows = chunk * n_chunks
    q_blk = ROPE_Q * D_MODEL // RET_QK_DIM
    k_blk = ROPE_K * D_MODEL // RET_QK_DIM
    v_blk = LIN_V * D_MODEL // RET_V_DIM
    g_blk = SILU_G_RET * D_MODEL // RET_V_DIM
    return pl.pallas_call(
        functools.partial(_retention_kernel, chunk=chunk, n_chunks=n_chunks),
        grid=(RET_HEADS, s // rows),
        in_specs=[
            pl.BlockSpec((1, 8, 128), lambda h, n: (h, 0, 0)),
            pl.BlockSpec((rows, RET_QK_DIM), lambda h, n: (n, q_blk + h)),
            pl.BlockSpec((rows, RET_QK_DIM), lambda h, n: (n, k_blk + h)),
            pl.BlockSpec((rows, RET_V_DIM), lambda h, n: (n, v_blk + h)),
            pl.BlockSpec((rows, RET_V_DIM), lambda h, n: (n, g_blk + h)),
            pl.BlockSpec(memory_space=pl.ANY),
        ],
        out_specs=pl.BlockSpec((rows, RET_V_DIM), lambda h, n: (n, ACT_RET // RET_V_DIM + h)),
        out_shape=jax.ShapeDtypeStruct(acts.shape, acts.dtype),
        input_output_aliases={5: 0},
        scratch_shapes=[pltpu.VMEM((RET_QK_DIM, RET_V_DIM), F32)],
        compiler_params=_params("parallel", "arbitrary"),
        name="retention_branch",
    )(lg_tab, z_qk, z_qk, z_lin, z_silu, acts)


def _mem_attn_kernel(q_ref, gate_ref, k_ref, v_ref, acts_ref, o_ref):
    del acts_ref
    scores = lax.dot_general(q_ref[...], k_ref[...], (((1,), (1,)), ((), ())),
                             preferred_element_type=F32) * (MEM_HEAD_DIM ** -0.5)
    e = jnp.exp(scores - jnp.max(scores, axis=-1, keepdims=True))
    p = e * (1.0 / jnp.sum(e, axis=-1, keepdims=True))
    mo = jnp.dot(p.astype(BF16), v_ref[...], preferred_element_type=F32)
    o_ref[...] = (mo * gate_ref[...].astype(F32)).astype(o_ref.dtype)


def _mem_branch(z_lin, z_silu, km, vm, acts, tm):
    s = z_lin.shape[0]
    m = km.shape[0]
    hd = MEM_HEAD_DIM
    return pl.pallas_call(
        _mem_attn_kernel,
        grid=(MEM_HEADS, s // tm),
        in_specs=[
            pl.BlockSpec((tm, hd), lambda h, i: (i, LIN_Q_MEM * MEM_HEADS + h)),
            pl.BlockSpec((tm, hd), lambda h, i: (i, SILU_G_MEM * MEM_HEADS + h)),
            pl.BlockSpec((m, hd), lambda h, i: (0, h)),
            pl.BlockSpec((m, hd), lambda h, i: (0, h)),
            pl.BlockSpec(memory_space=pl.ANY),
        ],
        out_specs=pl.BlockSpec((tm, hd), lambda h, i: (i, ACT_MEM // hd + h)),
        out_shape=jax.ShapeDtypeStruct(acts.shape, acts.dtype),
        input_output_aliases={4: 0},
        compiler_params=_params("parallel", "arbitrary"),
        name="mem_branch",
    )(z_lin, z_silu, km, vm, acts)


def _merge_kernel(a_ref, w_ref, gate_ref, o_ref, acc_ref, *, sub_rows):
    kb = pl.program_id(2)
    last = pl.num_programs(2) - 1

    def accumulate(first, final):
        w = w_ref[...]
        for rows in _row_chunks(a_ref.shape[0], sub_rows):
            acc = jnp.dot(a_ref[rows, :], w, preferred_element_type=F32) * gate_ref[rows, :].astype(F32)
            if not first:
                acc = acc + acc_ref[rows, :]
            if final:
                o_ref[rows, :] = acc.astype(o_ref.dtype)
            else:
                acc_ref[rows, :] = acc

    @pl.when(kb == 0)
    def _():
        accumulate(True, False)

    @pl.when((kb > 0) & (kb < last))
    def _():
        accumulate(False, False)

    @pl.when(kb == last)
    def _():
        accumulate(False, True)


def _merge_branches(acts, w_cat, z_sig, tm, tn):
    s = acts.shape[0]
    n_chunks = ACT_WIDTH // D_MODEL
    nb = D_MODEL // tn

    def gate_block(kb, j):
        return ((kb + 1) // 2) * nb + j

    return pl.pallas_call(
        functools.partial(_merge_kernel, sub_rows=MXU_SUB_ROWS),
        grid=(s // tm, nb, n_chunks),
        in_specs=[pl.BlockSpec((tm, D_MODEL), lambda i, j, kb: (i, kb)),
                  pl.BlockSpec((D_MODEL, tn), lambda i, j, kb: (kb, j)),
                  pl.BlockSpec((tm, tn), lambda i, j, kb: (i, gate_block(kb, j)))],
        out_specs=pl.BlockSpec((tm, tn), lambda i, j, kb: (i, j)),
        out_shape=jax.ShapeDtypeStruct((s, D_MODEL), BF16),
        scratch_shapes=[pltpu.VMEM((tm, tn), F32)],
        compiler_params=_params("parallel", "parallel", "arbitrary"),
        name="merge_branches",
    )(acts, w_cat, z_sig)


def _out_proj_kernel(m_ref, w_ref, x_ref, g_ref, o_ref, *, sub_rows):
    for r in range(m_ref.shape[0] // sub_rows):
        rows = slice(r * sub_rows, (r + 1) * sub_rows)
        y = x_ref[rows, :] + jnp.dot(m_ref[rows, :], w_ref[...], preferred_element_type=F32)
        y = y * lax.rsqrt(jnp.mean(y * y, axis=-1, keepdims=True) + NORM_EPS)
        o_ref[rows, :] = y * g_ref[...]


def _out_proj(merged, w, x, g, tm):
    s, d = x.shape
    return pl.pallas_call(
        functools.partial(_out_proj_kernel, sub_rows=128),
        grid=(s // tm,),
        in_specs=[pl.BlockSpec((tm, d), lambda i: (i, 0)),
                  pl.BlockSpec((d, d), lambda i: (0, 0), pipeline_mode=pl.Buffered(1)),
                  pl.BlockSpec((tm, d), lambda i: (i, 0)),
                  pl.BlockSpec((1, d), lambda i: (0, 0))],
        out_specs=pl.BlockSpec((tm, d), lambda i: (i, 0)),
        out_shape=jax.ShapeDtypeStruct((s, d), F32),
        compiler_params=_params("arbitrary"),
        name="out_proj_norm",
    )(merged, w, x, g.reshape(1, d))


def kernel(x, mem, norm_in, norm_mem, w_in, w_pool_group, pool_scale, w_mem_k, w_mem_v,
           w_proj_pool, w_proj_ret, w_proj_mem, w_out, norm_f):
    b, s, d = x.shape
    depth = w_in.shape[0]
    assert b == 1 and d == D_MODEL and depth == 1

    pos = jnp.arange(s, dtype=F32)
    inv = ROPE_BASE ** (-jnp.arange(ROPE_HALF, dtype=F32) / ROPE_HALF)
    ang = pos[:, None] * inv[None, :]
    cos, sin = jnp.cos(ang), jnp.sin(ang)
    lg = jnp.log1p(-(2.0 ** (-5.0 - jnp.arange(RET_HEADS, dtype=F32))))
    lg_tab = jnp.broadcast_to(lg[:, None, None], (RET_HEADS, 8, 128))

    xs = x[0]
    mems = mem[0]
    gd = POOL_GROUP_DIM
    tm_in = 2048
    first = dict(s=s, tm=tm_in, tn=512, row_blocks=1)
    rest = dict(s=s, tm=tm_in, tn=1024, row_blocks=s // tm_in - 1, first_row_block=1)
    h_first = _rmsnorm(xs, norm_in[0], tm=512, rows=tm_in)
    norm_slab = 256
    z_lin, wb_lin, h_rest = _in_proj(
        h_first, w_in[0], LIN_SEGS, "linear", None, None, **first,
        casts=(_Cast(xs, s - tm_in, gain=norm_in[0].reshape(1, d), src_offset=tm_in, rows=s - tm_in,
                     n_slabs=(s - tm_in) // norm_slab),))
    z_lin, w_cat = _in_proj(h_rest, wb_lin, LIN_SEGS, "linear", None, None, **rest, z=z_lin,
                            casts=(_Cast(w_proj_ret[0], ACT_WIDTH, ACT_RET),))
    z_silu, wb_silu, wb_mem_v = _in_proj(h_first, w_in[0], SILU_SEGS, "silu", None, None, **first,
                                         casts=(_Cast(w_mem_v[0], D_MODEL),))
    z_silu, w_cat, wb_mem_k = _in_proj(h_rest, wb_silu, SILU_SEGS, "silu", None, None, **rest, z=z_silu,
                                       casts=(_Cast(w_proj_pool[0], ACT_WIDTH, ACT_POOL, w_cat),
                                              _Cast(w_mem_k[0], D_MODEL)))
    z_sig, wb_sig = _in_proj(h_first, w_in[0], SIG_SEGS, "sigmoid", None, None, **first)
    z_sig, w_cat, wb_out = _in_proj(h_rest, wb_sig, SIG_SEGS, "sigmoid", None, None, **rest, z=z_sig,
                                    casts=(_Cast(w_proj_mem[0], ACT_WIDTH, ACT_MEM, w_cat),
                                           _Cast(w_out[0], D_MODEL)))
    z_qk, wb_qk = _in_proj(h_first, w_in[0], ROPE_SEGS, "rope", cos, sin, **first)
    z_qk, wb_group = _in_proj(h_rest, wb_qk, ROPE_SEGS, "rope", cos, sin, **rest, z=z_qk,
                              casts=(_Cast(w_pool_group[0].reshape(POOL_GROUPS * gd, gd), POOL_GROUPS * gd),))

    acts = _pool_branch(z_lin, z_silu, wb_group.reshape(POOL_GROUPS, gd, gd), pool_scale[0], tm=1024)
    acts = _retention_branch(z_qk, z_lin, z_silu, lg_tab, acts, chunk=256, n_chunks=16)

    memn = _rmsnorm(mems, norm_mem[0], tm=256)
    km = _matmul(memn, wb_mem_k, tm=256, tn=1024)
    vm = _matmul(memn, wb_mem_v, tm=256, tn=1024)
    acts = _mem_branch(z_lin, z_silu, km, vm, acts, tm=2048)

    merged = _merge_branches(acts, w_cat, z_sig, tm=1024, tn=1024)
    out = _out_proj(merged, wb_out, xs, norm_f, tm=256)
    return out[None]
```

```python
import functools
from typing import NamedTuple, Optional

import jax
import jax.numpy as jnp
from jax import lax
from jax.experimental import pallas as pl
from jax.experimental.pallas import tpu as pltpu

D_MODEL = 4096
POOL_WINDOWS = (2, 4, 8, 16)
POOL_GROUPS = 4
POOL_GROUP_DIM = D_MODEL // POOL_GROUPS
POOL_HALO = 32
RET_QK_DIM = 256
RET_HEADS = D_MODEL // RET_QK_DIM
RET_V_DIM = 2 * RET_QK_DIM
RET_V_WIDTH = RET_HEADS * RET_V_DIM
MEM_HEADS = 4
MEM_HEAD_DIM = D_MODEL // MEM_HEADS
ROPE_BASE = 10000.0
ROPE_HALF = RET_QK_DIM // 2
NORM_EPS = 1e-6

SEG_U_POOL, SEG_G_POOL, SEG_Q, SEG_K, SEG_V, SEG_G_RET, SEG_Q_MEM, SEG_G_MEM, SEG_A_POOL, SEG_A_RET, SEG_A_MEM = (
    0, 1, 2, 3, 4, 6, 8, 9, 10, 11, 12)
LIN_SEGS = (SEG_U_POOL, SEG_V, SEG_V + 1, SEG_Q_MEM, SEG_A_POOL, SEG_A_RET, SEG_A_MEM)
LIN_A_POOL = 4
LIN_U_POOL, LIN_V, LIN_Q_MEM = 0, 1, 3
SILU_SEGS = (SEG_G_POOL, SEG_G_RET, SEG_G_RET + 1, SEG_G_MEM)
SILU_G_POOL, SILU_G_RET, SILU_G_MEM = 0, 1, 3
ROPE_SEGS = (SEG_Q, SEG_K)
ROPE_Q, ROPE_K = 0, 1
ACT_POOL, ACT_RET, ACT_MEM = 0, D_MODEL, D_MODEL + RET_V_WIDTH
ACT_WIDTH = 2 * D_MODEL + RET_V_WIDTH

F32 = jnp.float32
BF16 = jnp.bfloat16
VMEM_LIMIT = 60 * 1024 * 1024
MXU_SUB_ROWS = 256


def _params(*sem):
    return pltpu.CompilerParams(dimension_semantics=sem, vmem_limit_bytes=VMEM_LIMIT)


def _sigmoid(x):
    return 0.5 * jnp.tanh(0.5 * x) + 0.5


def _silu(x):
    h = 0.5 * x
    return h + h * jnp.tanh(h)


def _row_chunks(total, size):
    chunks = [slice(a, a + size) for a in range(0, total - size, size)]
    half = size // 2
    return chunks + [slice(total - size, total - half), slice(total - half, total)]


def _rmsnorm_kernel(x_ref, g_ref, o_ref):
    x = x_ref[...]
    y = x * lax.rsqrt(jnp.mean(x * x, axis=-1, keepdims=True) + NORM_EPS)
    o_ref[...] = (y * g_ref[...]).astype(o_ref.dtype)


def _rmsnorm(x, g, tm, rows=None):
    d = x.shape[1]
    rows = rows or x.shape[0]
    return pl.pallas_call(
        _rmsnorm_kernel,
        grid=(rows // tm,),
        in_specs=[pl.BlockSpec((tm, d), lambda i: (i, 0)),
                  pl.BlockSpec((1, d), lambda i: (0, 0))],
        out_specs=pl.BlockSpec((tm, d), lambda i: (i, 0)),
        out_shape=jax.ShapeDtypeStruct((rows, d), BF16),
        compiler_params=_params("parallel"),
        name="rmsnorm",
    )(x, g.reshape(1, d))


class _Cast(NamedTuple):
    src: jax.Array
    dst_rows: int
    dst_offset: int = 0
    dst: Optional[jax.Array] = None
    gain: Optional[jax.Array] = None
    src_offset: int = 0
    rows: Optional[int] = None
    n_slabs: Optional[int] = None


def _in_proj_kernel(*refs, epilogue, tn, sub_rows, cast_has_gain, n_alias, emit_w):
    h_ref, w_ref = refs[:2]
    n_in = 2 + (2 if epilogue == "rope" else 0)
    cast_in, gains = [], []
    for has_gain in cast_has_gain:
        cast_in.append(refs[n_in])
        gains.append(refs[n_in + 1] if has_gain else None)
        n_in += 2 if has_gain else 1
    outs = refs[n_in + n_alias:]
    o_ref = outs[0]
    cast_out = outs[1 + emit_w:]
    if emit_w:
        outs[1][...] = w_ref[...].astype(BF16)
        w_ref = outs[1]
    for rows in _row_chunks(h_ref.shape[0], sub_rows):
        acc = jnp.dot(h_ref[rows, :], w_ref[...], preferred_element_type=F32)
        if epilogue == "linear":
            o_ref[rows, :] = acc.astype(o_ref.dtype)
        elif epilogue == "silu":
            o_ref[rows, :] = _silu(acc).astype(o_ref.dtype)
        elif epilogue == "sigmoid":
            o_ref[rows, :] = _sigmoid(acc).astype(o_ref.dtype)
        else:
            assert epilogue == "rope"
            cos = refs[2][rows, :]
            sin = refs[3][rows, :]
            for hh in range(tn // RET_QK_DIM):
                lo = hh * RET_QK_DIM
                x1 = acc[:, lo:lo + ROPE_HALF]
                x2 = acc[:, lo + ROPE_HALF:lo + RET_QK_DIM]
                o_ref[rows, lo:lo + ROPE_HALF] = (x1 * cos - x2 * sin).astype(o_ref.dtype)
                o_ref[rows, lo + ROPE_HALF:lo + RET_QK_DIM] = (x2 * cos + x1 * sin).astype(o_ref.dtype)
    for src_ref, gain_ref, dst_ref in zip(cast_in, gains, cast_out):
        v = src_ref[...]
        if gain_ref is not None:
            v = v * lax.rsqrt(jnp.mean(v * v, axis=-1, keepdims=True) + NORM_EPS) * gain_ref[...]
        dst_ref[...] = v.astype(dst_ref.dtype)


def _in_proj(h, w, segs, epilogue, cos, sin, *, s, tm, tn, row_blocks, first_row_block=0, z=None, casts=()):
    d = h.shape[1]
    assert h.shape[0] == row_blocks * tm
    nb = D_MODEL // tn
    nj = len(segs) * nb
    steps = row_blocks * nj
    emit_w = w.dtype == F32

    def w_block(j):
        if not emit_w:
            return j
        js = j // nb
        seg = segs[-1]
        for idx in range(len(segs) - 2, -1, -1):
            seg = jnp.where(js == idx, segs[idx], seg)
        return seg * nb + j % nb

    def row_block(i):
        return first_row_block + i

    in_specs = [pl.BlockSpec((tm, d), lambda i, j: (i, 0), pipeline_mode=pl.Buffered(1)),
                pl.BlockSpec((d, tn), lambda i, j: (0, w_block(j)))]
    args = [h, w]
    if epilogue == "rope":
        in_specs += [pl.BlockSpec((tm, ROPE_HALF), lambda i, j: (row_block(i), 0))] * 2
        args += [cos, sin]
    out_specs = [pl.BlockSpec((tm, tn), lambda i, j: (row_block(i), j))]
    out_shape = [jax.ShapeDtypeStruct((s, len(segs) * D_MODEL), BF16)]
    if emit_w:
        out_specs.append(pl.BlockSpec((d, tn), lambda i, j: (0, j)))
        out_shape.append(jax.ShapeDtypeStruct((d, len(segs) * D_MODEL), BF16))
    for c in casts:
        n_slabs = c.n_slabs or 1 << (steps.bit_length() - 1)
        assert n_slabs <= steps
        slab_rows, cols = (c.rows or c.src.shape[0]) // n_slabs, c.src.shape[1]
        src_first, dst_first = c.src_offset // slab_rows, c.dst_offset // slab_rows

        def slab(i, j, n_slabs=n_slabs):
            return jnp.minimum(i * nj + j, n_slabs - 1)

        in_specs.append(pl.BlockSpec((slab_rows, cols), lambda i, j, slab=slab, first=src_first: (first + slab(i, j), 0)))
        out_specs.append(pl.BlockSpec((slab_rows, cols), lambda i, j, slab=slab, first=dst_first: (first + slab(i, j), 0)))
        out_shape.append(jax.ShapeDtypeStruct((c.dst_rows, cols), BF16))
        args.append(c.src)
        if c.gain is not None:
            in_specs.append(pl.BlockSpec((1, cols), lambda i, j: (0, 0)))
            args.append(c.gain)
    n_cast_args = len(args)
    aliases = {}
    if z is not None:
        aliases[len(args)] = 0
        in_specs.append(pl.BlockSpec(memory_space=pl.ANY))
        args.append(z)
    for out_idx, c in enumerate(casts, start=1 + emit_w):
        if c.dst is not None:
            aliases[len(args)] = out_idx
            in_specs.append(pl.BlockSpec(memory_space=pl.ANY))
            args.append(c.dst)
    return pl.pallas_call(
        functools.partial(_in_proj_kernel, epilogue=epilogue, tn=tn, sub_rows=MXU_SUB_ROWS,
                          cast_has_gain=tuple(c.gain is not None for c in casts),
                          n_alias=len(args) - n_cast_args, emit_w=emit_w),
        grid=(row_blocks, nj),
        in_specs=in_specs,
        out_specs=out_specs,
        out_shape=out_shape,
        input_output_aliases=aliases,
        compiler_params=_params("parallel", "arbitrary"),
        name="in_proj_" + epilogue + ("_first" if emit_w else "_rest"),
    )(*args)


def _matmul_kernel(a_ref, b_ref, o_ref):
    o_ref[...] = jnp.dot(a_ref[...], b_ref[...], preferred_element_type=F32).astype(o_ref.dtype)


def _matmul(a, b, tm, tn):
    m, k = a.shape
    n = b.shape[1]
    return pl.pallas_call(
        _matmul_kernel,
        grid=(m // tm, n // tn),
        in_specs=[pl.BlockSpec((tm, k), lambda i, j: (i, 0)),
                  pl.BlockSpec((k, tn), lambda i, j: (0, j))],
        out_specs=pl.BlockSpec((tm, tn), lambda i, j: (i, j)),
        out_shape=jax.ShapeDtypeStruct((m, n), BF16),
        compiler_params=_params("parallel", "arbitrary"),
        name="mem_kv_proj",
    )(a, b)


def _pool_kernel(u_ref, halo_ref, gate_ref, w_ref, scale_ref, o_ref, buf_a, buf_b, *, tm):
    g = pl.program_id(0)
    i = pl.program_id(1)
    u = u_ref[...].astype(F32)
    halo = halo_ref[...].astype(F32)
    end = POOL_HALO + tm
    buf_a[0:POOL_HALO, :] = jnp.where(i == 0, 0.0, halo)
    buf_a[POOL_HALO:end, :] = u
    t = i * tm + lax.broadcasted_iota(jnp.int32, (tm, 1), 0)

    for gi, win in enumerate(POOL_WINDOWS):
        @pl.when(g == gi)
        def _(win=win):
            src, dst = buf_a, buf_b
            span, lo = 1, 8
            while span < win:
                dst[lo:end, :] = src[lo:end, :] + src[lo - span:end - span, :]
                src, dst = dst, src
                span, lo = 2 * span, lo + 8
            wsum = src[POOL_HALO:end, :]
            cnt = jnp.minimum(t + 1, win).astype(F32)
            mixed = wsum / cnt - u
            y = jnp.dot(mixed.astype(BF16), w_ref[0], preferred_element_type=F32)
            y = y * scale_ref[...] * gate_ref[...].astype(F32)
            o_ref[...] = y.astype(o_ref.dtype)


def _pool_branch(z_lin, z_silu, w_group, scale, tm):
    s = z_lin.shape[0]
    gd = POOL_GROUP_DIM
    halo_blocks_per_tile = tm // POOL_HALO
    return pl.pallas_call(
        functools.partial(_pool_kernel, tm=tm),
        grid=(POOL_GROUPS, s // tm),
        in_specs=[
            pl.BlockSpec((tm, gd), lambda g, i: (i, LIN_U_POOL * POOL_GROUPS + g)),
            pl.BlockSpec((POOL_HALO, gd),
                         lambda g, i: (jnp.maximum(i * halo_blocks_per_tile - 1, 0), LIN_U_POOL * POOL_GROUPS + g)),
            pl.BlockSpec((tm, gd), lambda g, i: (i, SILU_G_POOL * POOL_GROUPS + g)),
            pl.BlockSpec((1, gd, gd), lambda g, i: (g, 0, 0)),
            pl.BlockSpec((1, gd), lambda g, i: (0, g)),
        ],
        out_specs=pl.BlockSpec((tm, gd), lambda g, i: (i, ACT_POOL // gd + g)),
        out_shape=jax.ShapeDtypeStruct((s, ACT_WIDTH), BF16),
        scratch_shapes=[pltpu.VMEM((tm + POOL_HALO, gd), F32)] * 2,
        compiler_params=_params("parallel", "arbitrary"),
        name="pool_branch",
    )(z_lin, z_lin, z_silu, w_group, scale.reshape(1, D_MODEL))


def _retention_kernel(lg_ref, q_ref, k_ref, v_ref, gate_ref, acts_ref, o_ref, state_ref, *, chunk, n_chunks):
    del acts_ref
    @pl.when(pl.program_id(1) == 0)
    def _():
        state_ref[...] = jnp.zeros_like(state_ref)

    k_scale = RET_QK_DIM ** -0.5
    lg = lg_ref[0][0:1, 0:1]
    row = lax.broadcasted_iota(jnp.int32, (chunk, 1), 0).astype(F32)
    col = lax.broadcasted_iota(jnp.int32, (1, chunk), 1).astype(F32)
    diff = row - col
    intra = jnp.where(diff >= 0, jnp.exp(jnp.maximum(diff, 0.0) * lg) * k_scale, 0.0)
    q_decay = jnp.exp((row + 1.0) * lg)
    k_decay = jnp.exp((chunk - 1.0 - row) * lg) * k_scale
    chunk_decay = jnp.exp(chunk * lg)

    for c in range(n_chunks):
        rows = slice(c * chunk, (c + 1) * chunk)
        q = q_ref[rows, :]
        k = k_ref[rows, :]
        v = v_ref[rows, :]
        state = state_ref[...]
        scores = lax.dot_general(q, k, (((1,), (1,)), ((), ())), preferred_element_type=F32) * intra
        inner = jnp.dot(scores.astype(BF16), v, preferred_element_type=F32)
        cross = jnp.dot(q, state.astype(BF16), preferred_element_type=F32) * q_decay
        kd = (k.astype(F32) * k_decay).astype(BF16)
        state_ref[...] = state * chunk_decay + lax.dot_general(
            kd, v, (((0,), (0,)), ((), ())), preferred_element_type=F32)

        o = inner + cross
        o = o * lax.rsqrt(jnp.mean(o * o, axis=-1, keepdims=True) + NORM_EPS)
        o_ref[rows, :] = (o * gate_ref[rows, :].astype(F32)).astype(o_ref.dtype)


def _retention_branch(z_qk, z_lin, z_silu, lg_tab, acts, chunk, n_chunks):
    s = z_qk.shape[0]
    rows = chunk * n_chunks
    q_blk = ROPE_Q * D_MODEL // RET_QK_DIM
    k_blk = ROPE_K * D_MODEL // RET_QK_DIM
    v_blk = LIN_V * D_MODEL // RET_V_DIM
    g_blk = SILU_G_RET * D_MODEL // RET_V_DIM
    return pl.pallas_call(
        functools.partial(_retention_kernel, chunk=chunk, n_chunks=n_chunks),
        grid=(RET_HEADS, s // rows),
        in_specs=[
            pl.BlockSpec((1, 8, 128), lambda h, n: (h, 0, 0)),
            pl.BlockSpec((rows, RET_QK_DIM), lambda h, n: (n, q_blk + h)),
            pl.BlockSpec((rows, RET_QK_DIM), lambda h, n: (n, k_blk + h)),
            pl.BlockSpec((rows, RET_V_DIM), lambda h, n: (n, v_blk + h)),
            pl.BlockSpec((rows, RET_V_DIM), lambda h, n: (n, g_blk + h)),
            pl.BlockSpec(memory_space=pl.ANY),
        ],
        out_specs=pl.BlockSpec((rows, RET_V_DIM), lambda h, n: (n, ACT_RET // RET_V_DIM + h)),
        out_shape=jax.ShapeDtypeStruct(acts.shape, acts.dtype),
        input_output_aliases={5: 0},
        scratch_shapes=[pltpu.VMEM((RET_QK_DIM, RET_V_DIM), F32)],
        compiler_params=_params("parallel", "arbitrary"),
        name="retention_branch",
    )(lg_tab, z_qk, z_qk, z_lin, z_silu, acts)


def _mem_attn_kernel(q_ref, gate_ref, k_ref, v_ref, acts_ref, o_ref):
    del acts_ref
    scores = lax.dot_general(q_ref[...], k_ref[...], (((1,), (1,)), ((), ())),
                             preferred_element_type=F32) * (MEM_HEAD_DIM ** -0.5)
    e = jnp.exp(scores - jnp.max(scores, axis=-1, keepdims=True))
    p = e * (1.0 / jnp.sum(e, axis=-1, keepdims=True))
    mo = jnp.dot(p.astype(BF16), v_ref[...], preferred_element_type=F32)
    o_ref[...] = (mo * gate_ref[...].astype(F32)).astype(o_ref.dtype)


def _mem_branch(z_lin, z_silu, km, vm, acts, tm):
    s = z_lin.shape[0]
    m = km.shape[0]
    hd = MEM_HEAD_DIM
    return pl.pallas_call(
        _mem_attn_kernel,
        grid=(MEM_HEADS, s // tm),
        in_specs=[
            pl.BlockSpec((tm, hd), lambda h, i: (i, LIN_Q_MEM * MEM_HEADS + h)),
            pl.BlockSpec((tm, hd), lambda h, i: (i, SILU_G_MEM * MEM_HEADS + h)),
            pl.BlockSpec((m, hd), lambda h, i: (0, h)),
            pl.BlockSpec((m, hd), lambda h, i: (0, h)),
            pl.BlockSpec(memory_space=pl.ANY),
        ],
        out_specs=pl.BlockSpec((tm, hd), lambda h, i: (i, ACT_MEM // hd + h)),
        out_shape=jax.ShapeDtypeStruct(acts.shape, acts.dtype),
        input_output_aliases={4: 0},
        compiler_params=_params("parallel", "arbitrary"),
        name="mem_branch",
    )(z_lin, z_silu, km, vm, acts)


def _merge_kernel(a_ref, w_ref, gate_ref, o_ref, acc_ref, *, sub_rows):
    kb = pl.program_id(2)
    last = pl.num_programs(2) - 1

    def accumulate(first, final):
        w = w_ref[...]
        for rows in _row_chunks(a_ref.shape[0], sub_rows):
            gate = _sigmoid(gate_ref[rows, :].astype(F32))
            acc = jnp.dot(a_ref[rows, :], w, preferred_element_type=F32) * gate
            if not first:
                acc = acc + acc_ref[rows, :]
            if final:
                o_ref[rows, :] = acc.astype(o_ref.dtype)
            else:
                acc_ref[rows, :] = acc

    @pl.when(kb == 0)
    def _():
        accumulate(True, False)

    @pl.when((kb > 0) & (kb < last))
    def _():
        accumulate(False, False)

    @pl.when(kb == last)
    def _():
        accumulate(False, True)


def _merge_branches(acts, w_cat, z_lin, tm, tn):
    s = acts.shape[0]
    n_chunks = ACT_WIDTH // D_MODEL
    nb = D_MODEL // tn

    def gate_block(kb, j):
        return (LIN_A_POOL + (kb + 1) // 2) * nb + j

    return pl.pallas_call(
        functools.partial(_merge_kernel, sub_rows=MXU_SUB_ROWS),
        grid=(s // tm, nb, n_chunks),
        in_specs=[pl.BlockSpec((tm, D_MODEL), lambda i, j, kb: (i, kb)),
                  pl.BlockSpec((D_MODEL, tn), lambda i, j, kb: (kb, j)),
                  pl.BlockSpec((tm, tn), lambda i, j, kb: (i, gate_block(kb, j)))],
        out_specs=pl.BlockSpec((tm, tn), lambda i, j, kb: (i, j)),
        out_shape=jax.ShapeDtypeStruct((s, D_MODEL), BF16),
        scratch_shapes=[pltpu.VMEM((tm, tn), F32)],
        compiler_params=_params("parallel", "parallel", "arbitrary"),
        name="merge_branches",
    )(acts, w_cat, z_lin)


def _out_proj_kernel(m_ref, w_ref, x_ref, g_ref, o_ref, *, sub_rows):
    for r in range(m_ref.shape[0] // sub_rows):
        rows = slice(r * sub_rows, (r + 1) * sub_rows)
        y = x_ref[rows, :] + jnp.dot(m_ref[rows, :], w_ref[...], preferred_element_type=F32)
        y = y * lax.rsqrt(jnp.mean(y * y, axis=-1, keepdims=True) + NORM_EPS)
        o_ref[rows, :] = y * g_ref[...]


def _out_proj(merged, w, x, g, tm):
    s, d = x.shape
    return pl.pallas_call(
        functools.partial(_out_proj_kernel, sub_rows=128),
        grid=(s // tm,),
        in_specs=[pl.BlockSpec((tm, d), lambda i: (i, 0)),
                  pl.BlockSpec((d, d), lambda i: (0, 0), pipeline_mode=pl.Buffered(1)),
                  pl.BlockSpec((tm, d), lambda i: (i, 0)),
                  pl.BlockSpec((1, d), lambda i: (0, 0))],
        out_specs=pl.BlockSpec((tm, d), lambda i: (i, 0)),
        out_shape=jax.ShapeDtypeStruct((s, d), F32),
        compiler_params=_params("arbitrary"),
        name="out_proj_norm",
    )(merged, w, x, g.reshape(1, d))


def kernel(x, mem, norm_in, norm_mem, w_in, w_pool_group, pool_scale, w_mem_k, w_mem_v,
           w_proj_pool, w_proj_ret, w_proj_mem, w_out, norm_f):
    b, s, d = x.shape
    depth = w_in.shape[0]
    assert b == 1 and d == D_MODEL and depth == 1

    pos = jnp.arange(s, dtype=F32)
    inv = ROPE_BASE ** (-jnp.arange(ROPE_HALF, dtype=F32) / ROPE_HALF)
    ang = pos[:, None] * inv[None, :]
    cos, sin = jnp.cos(ang), jnp.sin(ang)
    lg = jnp.log1p(-(2.0 ** (-5.0 - jnp.arange(RET_HEADS, dtype=F32))))
    lg_tab = jnp.broadcast_to(lg[:, None, None], (RET_HEADS, 8, 128))

    xs = x[0]
    mems = mem[0]
    gd = POOL_GROUP_DIM
    tm_in = 2048
    first = dict(s=s, tm=tm_in, tn=512, row_blocks=1)
    rest = dict(s=s, tm=tm_in, tn=1024, row_blocks=s // tm_in - 1, first_row_block=1)
    h_first = _rmsnorm(xs, norm_in[0], tm=512, rows=tm_in)
    norm_slab = 128
    z_lin, wb_lin, h_rest = _in_proj(
        h_first, w_in[0], LIN_SEGS, "linear", None, None, **first,
        casts=(_Cast(xs, s - tm_in, gain=norm_in[0].reshape(1, d), src_offset=tm_in, rows=s - tm_in,
                     n_slabs=(s - tm_in) // norm_slab),))
    z_lin, w_cat, wb_out = _in_proj(h_rest, wb_lin, LIN_SEGS, "linear", None, None, **rest, z=z_lin,
                                    casts=(_Cast(w_proj_ret[0], ACT_WIDTH, ACT_RET),
                                           _Cast(w_out[0], D_MODEL)))
    z_silu, wb_silu, wb_mem_v, w_cat = _in_proj(h_first, w_in[0], SILU_SEGS, "silu", None, None, **first,
                                                casts=(_Cast(w_mem_v[0], D_MODEL),
                                                       _Cast(w_proj_mem[0], ACT_WIDTH, ACT_MEM, w_cat)))
    z_silu, w_cat, wb_mem_k = _in_proj(h_rest, wb_silu, SILU_SEGS, "silu", None, None, **rest, z=z_silu,
                                       casts=(_Cast(w_proj_pool[0], ACT_WIDTH, ACT_POOL, w_cat),
                                              _Cast(w_mem_k[0], D_MODEL)))
    z_qk, wb_qk = _in_proj(h_first, w_in[0], ROPE_SEGS, "rope", cos, sin, **first)
    z_qk, wb_group = _in_proj(h_rest, wb_qk, ROPE_SEGS, "rope", cos, sin, **rest, z=z_qk,
                              casts=(_Cast(w_pool_group[0].reshape(POOL_GROUPS * gd, gd), POOL_GROUPS * gd),))

    acts = _pool_branch(z_lin, z_silu, wb_group.reshape(POOL_GROUPS, gd, gd), pool_scale[0], tm=1024)
    acts = _retention_branch(z_qk, z_lin, z_silu, lg_tab, acts, chunk=256, n_chunks=16)

    memn = _rmsnorm(mems, norm_mem[0], tm=256)
    km = _matmul(memn, wb_mem_k, tm=256, tn=1024)
    vm = _matmul(memn, wb_mem_v, tm=256, tn=1024)
    acts = _mem_branch(z_lin, z_silu, km, vm, acts, tm=2048)

    merged = _merge_branches(acts, w_cat, z_lin, tm=1024, tn=1024)
    out = _out_proj(merged, wb_out, xs, norm_f, tm=256)
    return out[None]
```
